```python
import jax, jax.numpy as jnp
from jax import lax
import numpy as np

D_MODEL = 4096
BATCH = 4
SEQ = 2048
DEPTH = 1

CONV_WIDTH = D_MODEL
CONV_K = 3
SSM_EXPAND = 2
SSM_D_INNER = SSM_EXPAND * D_MODEL
SSM_HEAD_DIM = 64
SSM_N_HEADS = SSM_D_INNER // SSM_HEAD_DIM
SSM_N_GROUPS = 8
SSM_D_STATE = 128
SSM_CONV_K = 4
SSM_CHUNK = 128
SSM_CONV_DIM = SSM_D_INNER + 2 * SSM_N_GROUPS * SSM_D_STATE
D_FF = 11008
FFN_RESIDUAL_SCALE = 0.5
RMS_EPS = 1e-5

IN_SPLITS = (CONV_WIDTH, CONV_WIDTH, CONV_WIDTH,
             SSM_D_INNER,
             SSM_CONV_DIM,
             SSM_N_HEADS,
             D_MODEL, D_MODEL)
D_IN_PROJ = CONV_WIDTH * 3 + SSM_D_INNER + SSM_CONV_DIM + SSM_N_HEADS + 2 * D_MODEL

kernel_name = "hybrid_conv_ssd_gated_macaron"


def _split_cols(t, sizes):
    out, start = [], 0
    for s in sizes:
        out.append(t[..., start:start + s])
        start += s
    return out


def _rmsnorm(x, w):
    x32 = x.astype(jnp.float32)
    y = x32 * lax.rsqrt(jnp.mean(x32 * x32, axis=-1, keepdims=True) + RMS_EPS)
    return (y * w.astype(jnp.float32)).astype(x.dtype)


def _swiglu(h, w_gate, w_up, w_down):
    return (jax.nn.silu(h @ w_gate) * (h @ w_up)) @ w_down


def _causal_depthwise_conv(u, w):
    k_taps = w.shape[0]
    length = u.shape[1]
    up = jnp.pad(u, ((0, 0), (k_taps - 1, 0), (0, 0)))
    y = up[:, 0:length] * w[0]
    for k in range(1, k_taps):
        y = y + up[:, k:k + length] * w[k]
    return y


def _ssd_chunked(xs, dt, a_neg, bm, cm):
    b, l, h, p = xs.shape
    g, n = bm.shape[2], bm.shape[3]
    r = h // g
    c = l // SSM_CHUNK
    T = SSM_CHUNK
    xdt = (xs * dt[..., None]).reshape(b, c, T, g, r, p)
    a = (dt * a_neg).reshape(b, c, T, g, r)
    a_cs = jnp.cumsum(a, axis=2)
    bc = bm.reshape(b, c, T, g, n)
    cc = cm.reshape(b, c, T, g, n)

    a_t = jnp.transpose(a_cs, (0, 1, 3, 4, 2))
    seg = a_t[..., :, None] - a_t[..., None, :]
    causal = jnp.tril(jnp.ones((T, T), dtype=bool))
    lmat = jnp.exp(jnp.where(causal, seg, -jnp.inf))
    cb = jnp.einsum('bcign,bcjgn->bcgij', cc, bc)
    scores = cb[:, :, :, None] * lmat
    y_diag = jnp.einsum('bcgrij,bcjgrp->bcigrp', scores, xdt)

    decay_to_end = jnp.exp(a_cs[:, :, -1:] - a_cs)
    states = jnp.einsum('bcjgn,bcjgrp->bcgrpn', bc, xdt * decay_to_end[..., None])

    chunk_decay = jnp.exp(a_cs[:, :, -1])

    def step(carry, inp):
        s_c, d_c = inp
        new = d_c[..., None, None] * carry + s_c
        return new, carry

    init = jnp.zeros((b, g, r, p, n), dtype=xs.dtype)
    _, prev_states = lax.scan(step, init, (jnp.moveaxis(states, 1, 0), jnp.moveaxis(chunk_decay, 1, 0)))
    prev_states = jnp.moveaxis(prev_states, 0, 1)

    y_off = jnp.einsum('bcign,bcgrpn->bcigrp', cc, prev_states) * jnp.exp(a_cs)[..., None]
    return (y_diag + y_off).reshape(b, l, h, p)


def _mixer_block(h, w_in, gate_bias, sconv_w, sconv_w_out, ssm_conv_w, ssm_conv_b,
                 ssm_dt_bias, ssm_A_log, ssm_D, ssm_norm, ssm_w_out, w_o):
    bsz, length, _ = h.shape
    proj = jnp.einsum('bld,de->ble', h, w_in)
    c_b, c_c, c_x, z, xbc, dt_raw, g_a, g_b = _split_cols(proj, IN_SPLITS)

    y_a = (c_b * _causal_depthwise_conv(c_c * c_x, sconv_w)) @ sconv_w_out

    xbc = jax.nn.silu(_causal_depthwise_conv(xbc, ssm_conv_w) + ssm_conv_b)
    xs, bm, cm = _split_cols(xbc, (SSM_D_INNER, SSM_N_GROUPS * SSM_D_STATE, SSM_N_GROUPS * SSM_D_STATE))
    xs = xs.reshape(bsz, length, SSM_N_HEADS, SSM_HEAD_DIM).astype(jnp.float32)
    bm = bm.reshape(bsz, length, SSM_N_GROUPS, SSM_D_STATE).astype(jnp.float32)
    cm = cm.reshape(bsz, length, SSM_N_GROUPS, SSM_D_STATE).astype(jnp.float32)
    dt = jax.nn.softplus(dt_raw.astype(jnp.float32) + ssm_dt_bias.astype(jnp.float32))
    a_neg = -jnp.exp(ssm_A_log.astype(jnp.float32))
    y = _ssd_chunked(xs, dt, a_neg, bm, cm) + ssm_D.astype(jnp.float32)[:, None] * xs
    y = y.reshape(bsz, length, SSM_D_INNER) * jax.nn.silu(z.astype(jnp.float32))
    yg = y.reshape(bsz, length, SSM_N_GROUPS, SSM_D_INNER // SSM_N_GROUPS)
    yg = yg * lax.rsqrt(jnp.mean(yg * yg, axis=-1, keepdims=True) + RMS_EPS)
    y = (yg.reshape(bsz, length, SSM_D_INNER) * ssm_norm.astype(jnp.float32)).astype(h.dtype)
    y_b = y @ ssm_w_out

    gates = jax.nn.sigmoid(jnp.concatenate([g_a, g_b], axis=-1) + gate_bias)
    gate_a, gate_b = _split_cols(gates, (D_MODEL, D_MODEL))
    return (gate_a * y_a + gate_b * y_b) @ w_o


def setup_inputs(seed: int = 0) -> dict:
    key = jax.random.key(seed)
    ks = jax.random.split(key, 24)
    f32 = jnp.float32
    nrm = lambda k, shape, scale: jax.random.normal(k, shape, f32) * scale
    gain = lambda k, shape: 1.0 + 0.02 * jax.random.normal(k, shape, f32)
    L = DEPTH
    dt_init = jnp.exp(jax.random.uniform(ks[11], (L, SSM_N_HEADS), f32, np.log(1e-3), np.log(1e-1)))
    return {
        "x": jax.random.normal(ks[0], (BATCH, SEQ, D_MODEL), f32),
        "ffn1_norm": gain(ks[1], (L, D_MODEL)),
        "ffn1_w_gate": nrm(ks[2], (L, D_MODEL, D_FF), D_MODEL ** -0.5),
        "ffn1_w_up": nrm(ks[3], (L, D_MODEL, D_FF), D_MODEL ** -0.5),
        "ffn1_w_down": nrm(ks[4], (L, D_FF, D_MODEL), D_FF ** -0.5),
        "mix_norm": gain(ks[5], (L, D_MODEL)),
        "w_in": nrm(ks[6], (L, D_MODEL, D_IN_PROJ), D_MODEL ** -0.5),
        "gate_bias": nrm(ks[7], (L, 2 * D_MODEL), 0.01),
        "sconv_w": nrm(ks[8], (L, CONV_K, CONV_WIDTH), CONV_K ** -0.5),
        "sconv_w_out": nrm(ks[9], (L, CONV_WIDTH, D_MODEL), CONV_WIDTH ** -0.5),
        "ssm_conv_w": nrm(ks[10], (L, SSM_CONV_K, SSM_CONV_DIM), SSM_CONV_K ** -0.5),
        "ssm_conv_b": nrm(ks[12], (L, SSM_CONV_DIM), 0.01),
        "ssm_dt_bias": dt_init + jnp.log(-jnp.expm1(-dt_init)),
        "ssm_A_log": jnp.log(jax.random.uniform(ks[13], (L, SSM_N_HEADS), f32, 1.0, 16.0)),
        "ssm_D": 1.0 + 0.1 * jax.random.normal(ks[14], (L, SSM_N_HEADS), f32),
        "ssm_norm": gain(ks[15], (L, SSM_D_INNER)),
        "ssm_w_out": nrm(ks[16], (L, SSM_D_INNER, D_MODEL), SSM_D_INNER ** -0.5),
        "w_o": nrm(ks[17], (L, D_MODEL, D_MODEL), D_MODEL ** -0.5),
        "ffn2_norm": gain(ks[18], (L, D_MODEL)),
        "ffn2_w_gate": nrm(ks[19], (L, D_MODEL, D_FF), D_MODEL ** -0.5),
        "ffn2_w_up": nrm(ks[20], (L, D_MODEL, D_FF), D_MODEL ** -0.5),
        "ffn2_w_down": nrm(ks[21], (L, D_FF, D_MODEL), D_FF ** -0.5),
        "final_norm": gain(ks[22], (D_MODEL,)),
    }


def reference(x, ffn1_norm, ffn1_w_gate, ffn1_w_up, ffn1_w_down, mix_norm, w_in, gate_bias,
              sconv_w, sconv_w_out, ssm_conv_w, ssm_conv_b, ssm_dt_bias, ssm_A_log, ssm_D,
              ssm_norm, ssm_w_out, w_o, ffn2_norm, ffn2_w_gate, ffn2_w_up, ffn2_w_down,
              final_norm):
    for i in range(DEPTH):
        x = x + FFN_RESIDUAL_SCALE * _swiglu(_rmsnorm(x, ffn1_norm[i]), ffn1_w_gate[i], ffn1_w_up[i], ffn1_w_down[i])
        h = _rmsnorm(x, mix_norm[i])
        x = x + _mixer_block(h, w_in[i], gate_bias[i], sconv_w[i], sconv_w_out[i], ssm_conv_w[i],
                             ssm_conv_b[i], ssm_dt_bias[i], ssm_A_log[i], ssm_D[i], ssm_norm[i],
                             ssm_w_out[i], w_o[i])
        x = x + FFN_RESIDUAL_SCALE * _swiglu(_rmsnorm(x, ffn2_norm[i]), ffn2_w_gate[i], ffn2_w_up[i], ffn2_w_down[i])
    return _rmsnorm(x, final_norm)
```

```python
import functools

import jax
import jax.numpy as jnp
from jax import lax
from jax.experimental import pallas as pl
from jax.experimental.pallas import tpu as pltpu

F32 = jnp.float32
BF16 = jnp.bfloat16

RMS_EPS = 1e-5
FFN_RESIDUAL_SCALE = 0.5
SSM_N_GROUPS = 8
SSM_D_STATE = 128
SSM_HEAD_DIM = 64
SSM_CHUNK = 128
LANES = 128
SUBLANES = 8
VMEM_LIMIT_BYTES = 56 * 1024 * 1024


def _params(*sem):
    return pltpu.CompilerParams(dimension_semantics=sem, vmem_limit_bytes=VMEM_LIMIT_BYTES)


def _tile(n, pref):
    if n <= pref:
        return n
    t = (pref // LANES) * LANES
    while t >= LANES:
        if n % t == 0:
            return t
        t -= LANES
    return n


def _silu(v):
    return v * jax.nn.sigmoid(v)


def _dot(a, b):
    return jnp.dot(a, b, preferred_element_type=F32)


def _rmsnorm_body(x_ref, w_ref, o_ref):
    x = x_ref[...]
    ms = jnp.mean(x * x, axis=-1, keepdims=True)
    o_ref[...] = (x * lax.rsqrt(ms + RMS_EPS) * w_ref[...]).astype(o_ref.dtype)


def _rmsnorm(x2d, w, out_dtype):
    m, d = x2d.shape
    tm = _tile(m, 256)
    return pl.pallas_call(
        _rmsnorm_body,
        grid=(m // tm,),
        in_specs=[pl.BlockSpec((tm, d), lambda i: (i, 0)), pl.BlockSpec((1, d), lambda i: (0, 0))],
        out_specs=pl.BlockSpec((tm, d), lambda i: (i, 0)),
        out_shape=jax.ShapeDtypeStruct((m, d), out_dtype),
        compiler_params=_params("parallel"),
        name="rmsnorm",
    )(x2d, w.reshape(1, d).astype(F32))


def _mm_tiles(m, k, n):
    if k <= 4096:
        return _tile(m, 1024), _tile(n, 512)
    if k <= 8192:
        return _tile(m, 512), _tile(n, 512)
    return _tile(m, 512), _tile(n, 256)


def _ffn_up_body(a_ref, wg_ref, wu_ref, o_ref):
    a = a_ref[...]
    g = _dot(a, wg_ref[...])
    u = _dot(a, wu_ref[...])
    o_ref[...] = (_silu(g) * u).astype(o_ref.dtype)


def _ffn_up(a, wg, wu):
    m, k = a.shape
    n = wg.shape[1]
    tm, tn = _tile(m, 1024), _tile(n, 256)
    return pl.pallas_call(
        _ffn_up_body,
        grid=(m // tm, n // tn),
        in_specs=[pl.BlockSpec((tm, k), lambda i, j: (i, 0)),
                  pl.BlockSpec((k, tn), lambda i, j: (0, j)),
                  pl.BlockSpec((k, tn), lambda i, j: (0, j))],
        out_specs=pl.BlockSpec((tm, tn), lambda i, j: (i, j)),
        out_shape=jax.ShapeDtypeStruct((m, n), BF16),
        compiler_params=_params("parallel", "arbitrary"),
        name="ffn_up",
    )(a, wg, wu)


def _mm_residual_body(a_ref, w_ref, r_ref, o_ref, *, scale):
    o_ref[...] = r_ref[...] + scale * _dot(a_ref[...], w_ref[...])


def _mm_residual(a, w, res, scale):
    m, k = a.shape
    n = w.shape[1]
    tm, tn = _mm_tiles(m, k, n)
    return pl.pallas_call(
        functools.partial(_mm_residual_body, scale=scale),
        grid=(m // tm, n // tn),
        in_specs=[pl.BlockSpec((tm, k), lambda i, j: (i, 0)),
                  pl.BlockSpec((k, tn), lambda i, j: (0, j)),
                  pl.BlockSpec((tm, tn), lambda i, j: (i, j))],
        out_specs=pl.BlockSpec((tm, tn), lambda i, j: (i, j)),
        out_shape=jax.ShapeDtypeStruct((m, n), F32),
        compiler_params=_params("parallel", "arbitrary"),
        name="mm_residual",
    )(a, w, res)


def _mm_plain_body(a_ref, w_ref, o_ref):
    o_ref[...] = _dot(a_ref[...], w_ref[...]).astype(o_ref.dtype)


def _mm_plain(a, w, out_dtype):
    m, k = a.shape
    n = w.shape[1]
    tm, tn = _mm_tiles(m, k, n)
    return pl.pallas_call(
        _mm_plain_body,
        grid=(m // tm, n // tn),
        in_specs=[pl.BlockSpec((tm, k), lambda i, j: (i, 0)),
                  pl.BlockSpec((k, tn), lambda i, j: (0, j))],
        out_specs=pl.BlockSpec((tm, tn), lambda i, j: (i, j)),
        out_shape=jax.ShapeDtypeStruct((m, n), out_dtype),
        compiler_params=_params("parallel", "arbitrary"),
        name="mm_plain",
    )(a, w)


def _mm_gate_body(a_ref, w_ref, g_ref, b_ref, o_ref):
    gate = jax.nn.sigmoid(g_ref[...].astype(F32) + b_ref[...])
    o_ref[...] = (gate * _dot(a_ref[...], w_ref[...])).astype(o_ref.dtype)


def _mm_gate_acc_body(a_ref, w_ref, g_ref, b_ref, p_ref, o_ref):
    gate = jax.nn.sigmoid(g_ref[...].astype(F32) + b_ref[...])
    o_ref[...] = (p_ref[...] + gate * _dot(a_ref[...], w_ref[...])).astype(o_ref.dtype)


def _mm_gate(a, w, gpre, gcol0, bias, prev, out_dtype):
    m, k = a.shape
    n = w.shape[1]
    tm, tn = _mm_tiles(m, k, n)
    joff = gcol0 // tn
    in_specs = [pl.BlockSpec((tm, k), lambda i, j: (i, 0)),
                pl.BlockSpec((k, tn), lambda i, j: (0, j)),
                pl.BlockSpec((tm, tn), lambda i, j: (i, j + joff)),
                pl.BlockSpec((1, tn), lambda i, j: (0, j))]
    args = [a, w, gpre, bias.reshape(1, n).astype(F32)]
    body = _mm_gate_body
    if prev is not None:
        in_specs.append(pl.BlockSpec((tm, tn), lambda i, j: (i, j)))
        args.append(prev)
        body = _mm_gate_acc_body
    return pl.pallas_call(
        body,
        grid=(m // tm, n // tn),
        in_specs=in_specs,
        out_specs=pl.BlockSpec((tm, tn), lambda i, j: (i, j)),
        out_shape=jax.ShapeDtypeStruct((m, n), out_dtype),
        compiler_params=_params("parallel", "arbitrary"),
        name="mm_gate",
    )(*args)


def _sconv_body(b_ref, c_ref, x_ref, w_ref, o_ref):
    u = c_ref[0].astype(F32) * x_ref[0].astype(F32)
    row = lax.broadcasted_iota(jnp.int32, u.shape, 0)
    w = w_ref[...]
    k_taps = w.shape[0]
    y = None
    for k in range(k_taps):
        shift = k_taps - 1 - k
        us = u if shift == 0 else jnp.where(row >= shift, pltpu.roll(u, shift, 0), 0.0)
        term = us * w[k:k + 1]
        y = term if y is None else y + term
    o_ref[0] = (b_ref[0].astype(F32) * y).astype(o_ref.dtype)


def _sconv(p3, w, width):
    bsz, length, _ = p3.shape
    tc = _tile(width, 256)
    nb = width // tc
    return pl.pallas_call(
        _sconv_body,
        grid=(bsz, nb),
        in_specs=[pl.BlockSpec((1, length, tc), lambda b, j: (b, 0, j)),
                  pl.BlockSpec((1, length, tc), lambda b, j: (b, 0, j + nb)),
                  pl.BlockSpec((1, length, tc), lambda b, j: (b, 0, j + 2 * nb)),
                  pl.BlockSpec((w.shape[0], tc), lambda b, j: (0, j))],
        out_specs=pl.BlockSpec((1, length, tc), lambda b, j: (b, 0, j)),
        out_shape=jax.ShapeDtypeStruct((bsz, length, width), BF16),
        compiler_params=_params("parallel", "parallel"),
        name="sconv",
    )(p3, p3, p3, w.astype(F32))


def _expand(v, e):
    hi = v.astype(BF16)
    r1 = v - hi.astype(F32)
    mid = r1.astype(BF16)
    lo = (r1 - mid.astype(F32)).astype(BF16)
    return _dot(hi, e) + _dot(mid, e) + _dot(lo, e)


def _ssd_body(x_ref, bm_ref, cm_ref, z_ref, dt_ref,
              wx_ref, wb_ref, wc_ref, bx_ref, bb_ref, bc_ref,
              dtb_ref, alog_ref, dx_ref, nw_ref, e1_ref, e2_ref,
              y_ref,
              state, halo_x, halo_b, halo_c, ext_x, ext_b, ext_c, dt_s, acs_s, acst_s, ybuf,
              *, heads_per_group):
    c = pl.program_id(1)
    g = pl.program_id(2)
    t = SSM_CHUNK
    r = heads_per_group
    rp = r * SSM_HEAD_DIM

    @pl.when(c == 0)
    def _():
        state[g] = jnp.zeros(state.shape[1:], F32)
        halo_x[g] = jnp.zeros(halo_x.shape[1:], F32)
        halo_b[g] = jnp.zeros(halo_b.shape[1:], F32)
        halo_c[g] = jnp.zeros(halo_c.shape[1:], F32)

    @pl.when(g == 0)
    def _():
        v = dt_ref[0] + dtb_ref[...]
        dt = jnp.maximum(v, 0.0) + jnp.log1p(jnp.exp(-jnp.abs(v)))
        a = dt * (-jnp.exp(alog_ref[...]))
        row = lax.broadcasted_iota(jnp.int32, a.shape, 0)
        acs = a
        s = 1
        while s < t:
            acs = acs + jnp.where(row >= s, pltpu.roll(acs, s, 0), 0.0)
            s *= 2
        dt_s[...] = dt
        acs_s[...] = acs
        acst = acs.T
        for gg in range(SSM_N_GROUPS):
            acst_s[gg] = acst[gg * r:(gg + 1) * r, :]

    dt = dt_s[...]
    acs = acs_s[...]
    eacs = jnp.exp(acs)
    dec = jnp.exp(acs[t - 1:t, :] - acs)
    e1 = e1_ref[...]
    dtx = _expand(dt, e1)
    decx = _expand(dec, e1)
    eacsx = _expand(eacs, e1)
    colb = _expand(acs, e2_ref[...])

    def conv(cur, halo, ext, w_ref, b_ref):
        ext[0:SUBLANES, :] = halo[g]
        ext[SUBLANES:SUBLANES + t, :] = cur
        w = w_ref[...]
        k_taps = w.shape[0]
        acc = None
        for k in range(k_taps):
            term = w[k:k + 1] * ext[pl.ds(SUBLANES - (k_taps - 1) + k, t), :]
            acc = term if acc is None else acc + term
        halo[g] = ext[t:t + SUBLANES, :]
        return _silu(acc + b_ref[...])

    xs = conv(x_ref[0].astype(F32), halo_x, ext_x, wx_ref, bx_ref)
    bmat = conv(bm_ref[0].astype(F32), halo_b, ext_b, wb_ref, bb_ref).astype(BF16)
    cmat = conv(cm_ref[0].astype(F32), halo_c, ext_c, wc_ref, bc_ref).astype(BF16)

    xdt = xs * dtx
    xdec_bf = (xdt * decx).astype(BF16)
    cb = lax.dot_general(cmat, bmat, (((1,), (1,)), ((), ())), preferred_element_type=F32)
    sprev = state[g]
    yoff = _dot(cmat, sprev.astype(BF16)) * eacsx
    snew = lax.dot_general(bmat, xdec_bf, (((0,), (0,)), ((), ())), preferred_element_type=F32)
    state[g] = eacsx[t - 1:t, :] * sprev + snew

    tri = (lax.broadcasted_iota(jnp.int32, (t, t), 0) >= lax.broadcasted_iota(jnp.int32, (t, t), 1))
    lane = lax.broadcasted_iota(jnp.int32, (t, LANES), 1)
    heads_per_tile = LANES // SSM_HEAD_DIM
    ss = jnp.zeros((t, 1), F32)
    for q in range(rp // LANES):
        sl = slice(q * LANES, (q + 1) * LANES)
        xdt_q = xdt[:, sl]
        acc = yoff[:, sl] + dx_ref[:, sl] * xs[:, sl]
        for e in range(heads_per_tile):
            k = q * heads_per_tile + e
            seg = colb[:, k * t:(k + 1) * t] - acst_s[g, pl.ds(k, 1), :]
            scores = (cb * jnp.exp(jnp.where(tri, seg, -jnp.inf))).astype(BF16)
            in_head = (lane >= e * SSM_HEAD_DIM) & (lane < (e + 1) * SSM_HEAD_DIM)
            acc = acc + _dot(scores, jnp.where(in_head, xdt_q, 0.0).astype(BF16))
        gated = acc * _silu(z_ref[0, :, sl].astype(F32))
        ybuf[:, sl] = gated
        ss = ss + jnp.sum(gated * gated, axis=-1, keepdims=True)
    inv = lax.rsqrt(ss / rp + RMS_EPS)
    y_ref[0] = (ybuf[...] * inv * nw_ref[...]).astype(y_ref.dtype)


def _ssd(xbc, z, dt_raw, conv_w, conv_b, dt_bias, a_log, d_skip, norm_w):
    bsz, length, d_inner = z.shape
    n_heads = dt_raw.shape[-1]
    g_, n_, t = SSM_N_GROUPS, SSM_D_STATE, SSM_CHUNK
    r = n_heads // g_
    rp = r * SSM_HEAD_DIM
    assert d_inner == n_heads * SSM_HEAD_DIM and rp % LANES == 0 and n_ == LANES and length % t == 0
    xb = d_inner // n_
    head_of_ch = jnp.arange(d_inner, dtype=jnp.int32) // SSM_HEAD_DIM
    e1 = (jnp.arange(n_heads, dtype=jnp.int32)[:, None] == head_of_ch[None, :]).astype(BF16)
    head_of_col = jnp.arange(n_heads * t, dtype=jnp.int32) // t
    e2 = (jnp.arange(n_heads, dtype=jnp.int32)[:, None] == head_of_col[None, :]).astype(BF16)
    d_x = jnp.repeat(d_skip.astype(F32), SSM_HEAD_DIM).reshape(1, d_inner)
    conv_w = conv_w.astype(F32)
    conv_b = conv_b.astype(F32).reshape(1, -1)
    kt = conv_w.shape[0]

    in_specs = [
        pl.BlockSpec((1, t, rp), lambda b, c, g: (b, c, g)),
        pl.BlockSpec((1, t, n_), lambda b, c, g: (b, c, xb + g)),
        pl.BlockSpec((1, t, n_), lambda b, c, g: (b, c, xb + g_ + g)),
        pl.BlockSpec((1, t, rp), lambda b, c, g: (b, c, g)),
        pl.BlockSpec((1, t, n_heads), lambda b, c, g: (b, c, 0)),
        pl.BlockSpec((kt, rp), lambda b, c, g: (0, g)),
        pl.BlockSpec((kt, n_), lambda b, c, g: (0, xb + g)),
        pl.BlockSpec((kt, n_), lambda b, c, g: (0, xb + g_ + g)),
        pl.BlockSpec((1, rp), lambda b, c, g: (0, g)),
        pl.BlockSpec((1, n_), lambda b, c, g: (0, xb + g)),
        pl.BlockSpec((1, n_), lambda b, c, g: (0, xb + g_ + g)),
        pl.BlockSpec((1, n_heads), lambda b, c, g: (0, 0)),
        pl.BlockSpec((1, n_heads), lambda b, c, g: (0, 0)),
        pl.BlockSpec((1, rp), lambda b, c, g: (0, g)),
        pl.BlockSpec((1, rp), lambda b, c, g: (0, g)),
        pl.BlockSpec((n_heads, rp), lambda b, c, g: (0, g)),
        pl.BlockSpec((n_heads, r * t), lambda b, c, g: (0, g)),
    ]
    scratch = [
        pltpu.VMEM((g_, n_, rp), F32),
        pltpu.VMEM((g_, SUBLANES, rp), F32),
        pltpu.VMEM((g_, SUBLANES, n_), F32),
        pltpu.VMEM((g_, SUBLANES, n_), F32),
        pltpu.VMEM((SUBLANES + t, rp), F32),
        pltpu.VMEM((SUBLANES + t, n_), F32),
        pltpu.VMEM((SUBLANES + t, n_), F32),
        pltpu.VMEM((t, n_heads), F32),
        pltpu.VMEM((t, n_heads), F32),
        pltpu.VMEM((g_, r, t), F32),
        pltpu.VMEM((t, rp), F32),
    ]
    return pl.pallas_call(
        functools.partial(_ssd_body, heads_per_group=r),
        grid=(bsz, length // t, g_),
        in_specs=in_specs,
        out_specs=pl.BlockSpec((1, t, rp), lambda b, c, g: (b, c, g)),
        out_shape=jax.ShapeDtypeStruct((bsz, length, d_inner), BF16),
        scratch_shapes=scratch,
        compiler_params=_params("arbitrary", "arbitrary", "arbitrary"),
        name="ssd",
    )(xbc, xbc, xbc, z, dt_raw, conv_w, conv_w, conv_w, conv_b, conv_b, conv_b,
      dt_bias.reshape(1, -1).astype(F32), a_log.reshape(1, -1).astype(F32), d_x,
      norm_w.reshape(1, -1).astype(F32), e1, e2)


def _swiglu_block(x2d, norm_w, w_gate, w_up, w_down):
    h = _rmsnorm(x2d, norm_w, BF16)
    act = _ffn_up(h, w_gate.astype(BF16), w_up.astype(BF16))
    return _mm_residual(act, w_down.astype(BF16), x2d, FFN_RESIDUAL_SCALE)


def _mixer_block(x2d, bsz, length, norm_w, w_in, gate_bias, sconv_w, sconv_w_out, ssm_conv_w, ssm_conv_b,
                 ssm_dt_bias, ssm_a_log, ssm_d, ssm_norm, ssm_w_out, w_o):
    d_model = x2d.shape[1]
    conv_width = sconv_w.shape[1]
    d_inner = ssm_norm.shape[0]
    conv_dim = ssm_conv_w.shape[1]
    n_heads = ssm_a_log.shape[0]
    o_z = 3 * conv_width
    o_xbc = o_z + d_inner
    o_dt = o_xbc + conv_dim
    o_g = o_dt + n_heads

    h = _rmsnorm(x2d, norm_w, BF16)
    p_conv = _mm_plain(h, w_in[:, :o_z].astype(BF16), BF16)
    p_z = _mm_plain(h, w_in[:, o_z:o_xbc].astype(BF16), BF16)
    p_xbc = _mm_plain(h, w_in[:, o_xbc:o_dt].astype(BF16), BF16)
    p_dt = _mm_plain(h, w_in[:, o_dt:o_g].astype(BF16), F32)
    p_gate = _mm_plain(h, w_in[:, o_g:].astype(BF16), BF16)

    v = _sconv(p_conv.reshape(bsz, length, o_z), sconv_w, conv_width).reshape(bsz * length, conv_width)
    y = _ssd(p_xbc.reshape(bsz, length, conv_dim), p_z.reshape(bsz, length, d_inner),
             p_dt.reshape(bsz, length, n_heads), ssm_conv_w, ssm_conv_b, ssm_dt_bias, ssm_a_log, ssm_d,
             ssm_norm).reshape(bsz * length, d_inner)

    merged = _mm_gate(v, sconv_w_out.astype(BF16), p_gate, 0, gate_bias[:d_model], None, F32)
    merged = _mm_gate(y, ssm_w_out.astype(BF16), p_gate, d_model, gate_bias[d_model:], merged, BF16)
    return _mm_residual(merged, w_o.astype(BF16), x2d, 1.0)


def kernel(x, ffn1_norm, ffn1_w_gate, ffn1_w_up, ffn1_w_down, mix_norm, w_in, gate_bias, sconv_w, sconv_w_out, ssm_conv_w, ssm_conv_b, ssm_dt_bias, ssm_A_log, ssm_D, ssm_norm, ssm_w_out, w_o, ffn2_norm, ffn2_w_gate, ffn2_w_up, ffn2_w_down, final_norm):
    bsz, length, d_model = x.shape
    x2d = x.reshape(bsz * length, d_model)
    for i in range(ffn1_norm.shape[0]):
        x2d = _swiglu_block(x2d, ffn1_norm[i], ffn1_w_gate[i], ffn1_w_up[i], ffn1_w_down[i])
        x2d = _mixer_block(x2d, bsz, length, mix_norm[i], w_in[i], gate_bias[i], sconv_w[i], sconv_w_out[i],
                           ssm_conv_w[i], ssm_conv_b[i], ssm_dt_bias[i], ssm_A_log[i], ssm_D[i], ssm_norm[i],
                           ssm_w_out[i], w_o[i])
        x2d = _swiglu_block(x2d, ffn2_norm[i], ffn2_w_gate[i], ffn2_w_up[i], ffn2_w_down[i])
    return _rmsnorm(x2d, final_norm, x.dtype).reshape(bsz, length, d_model)
```

```python
import functools

import jax
import jax.numpy as jnp
from jax import lax
from jax.experimental import pallas as pl
from jax.experimental.pallas import tpu as pltpu

F32 = jnp.float32
BF16 = jnp.bfloat16

RMS_EPS = 1e-5
FFN_RESIDUAL_SCALE = 0.5
SSM_N_GROUPS = 8
SSM_D_STATE = 128
SSM_HEAD_DIM = 64
SSM_CHUNK = 128
LANES = 128
SUBLANES = 8
VMEM_LIMIT_BYTES = 56 * 1024 * 1024
MM_WEIGHT_STREAM_BYTES = 16 * 1024 * 1024


def _params(*sem):
    return pltpu.CompilerParams(dimension_semantics=sem, vmem_limit_bytes=VMEM_LIMIT_BYTES)


def _tile(n, pref):
    if n <= pref:
        return n
    t = (pref // LANES) * LANES
    while t >= LANES:
        if n % t == 0:
            return t
        t -= LANES
    return n


def _silu(v):
    return v * jax.nn.sigmoid(v)


def _dot(a, b):
    return jnp.dot(a, b, preferred_element_type=F32)


def _rmsnorm_body(x_ref, w_ref, o_ref):
    x = x_ref[...]
    ms = jnp.mean(x * x, axis=-1, keepdims=True)
    o_ref[...] = (x * lax.rsqrt(ms + RMS_EPS) * w_ref[...]).astype(o_ref.dtype)


def _rmsnorm(x2d, w, out_dtype):
    m, d = x2d.shape
    tm = _tile(m, 256)
    return pl.pallas_call(
        _rmsnorm_body,
        grid=(m // tm,),
        in_specs=[pl.BlockSpec((tm, d), lambda i: (i, 0)), pl.BlockSpec((1, d), lambda i: (0, 0))],
        out_specs=pl.BlockSpec((tm, d), lambda i: (i, 0)),
        out_shape=jax.ShapeDtypeStruct((m, d), out_dtype),
        compiler_params=_params("parallel"),
        name="rmsnorm",
    )(x2d, w.reshape(1, d).astype(F32))


def _mm_tiles(m, k, n, w_itemsize, tm_pref=1024):
    tm = _tile(m, tm_pref)
    tn_budget = MM_WEIGHT_STREAM_BYTES // (2 * k * w_itemsize)
    tn = _tile(n, max(2 * LANES, min(512, (tn_budget // LANES) * LANES)))
    return tm, tn


def _resident(block, index_map):
    return pl.BlockSpec(block, index_map, pipeline_mode=pl.Buffered(1))


def _w_spec(k, tn, col0):
    if col0 % tn == 0:
        joff = col0 // tn
        return pl.BlockSpec((k, tn), lambda i, j: (0, j + joff))
    assert col0 % LANES == 0 and tn % LANES == 0
    return pl.BlockSpec((pl.Element(k), pl.Element(tn)),
                        lambda i, j: (0, pl.multiple_of(col0 + j * tn, LANES)))


def _wdot(a, w_ref):
    return _dot(a, w_ref[...].astype(BF16))


def _ffn_up_body(a_ref, wg_ref, wu_ref, o_ref):
    a = a_ref[...]
    g = _wdot(a, wg_ref)
    u = _wdot(a, wu_ref)
    o_ref[...] = (_silu(g) * u).astype(o_ref.dtype)


def _ffn_up(a, wg, wu):
    m, k = a.shape
    n = wg.shape[1]
    tm, tn = _tile(m, 2048), _tile(n, 256)
    return pl.pallas_call(
        _ffn_up_body,
        grid=(m // tm, n // tn),
        in_specs=[_resident((tm, k), lambda i, j: (i, 0)),
                  pl.BlockSpec((k, tn), lambda i, j: (0, j)),
                  pl.BlockSpec((k, tn), lambda i, j: (0, j))],
        out_specs=pl.BlockSpec((tm, tn), lambda i, j: (i, j)),
        out_shape=jax.ShapeDtypeStruct((m, n), BF16),
        compiler_params=_params("arbitrary", "arbitrary"),
        name="ffn_up",
    )(a, wg, wu)


def _mm_residual_body(a_ref, w_ref, r_ref, o_ref, *, scale):
    o_ref[...] = r_ref[...] + scale * _wdot(a_ref[...], w_ref)


def _mm_residual(a, w, res, scale):
    m, k = a.shape
    n = w.shape[1]
    tm, tn = _mm_tiles(m, k, n, w.dtype.itemsize)
    return pl.pallas_call(
        functools.partial(_mm_residual_body, scale=scale),
        grid=(m // tm, n // tn),
        in_specs=[_resident((tm, k), lambda i, j: (i, 0)),
                  pl.BlockSpec((k, tn), lambda i, j: (0, j)),
                  pl.BlockSpec((tm, tn), lambda i, j: (i, j))],
        out_specs=pl.BlockSpec((tm, tn), lambda i, j: (i, j)),
        out_shape=jax.ShapeDtypeStruct((m, n), F32),
        compiler_params=_params("arbitrary", "arbitrary"),
        name="mm_residual",
    )(a, w, res)


def _mm_plain_body(a_ref, w_ref, o_ref):
    o_ref[...] = _wdot(a_ref[...], w_ref).astype(o_ref.dtype)


def _mm_plain(a, w, col0, n, out_dtype):
    m, k = a.shape
    tm, tn = _mm_tiles(m, k, n, w.dtype.itemsize, tm_pref=2048)
    return pl.pallas_call(
        _mm_plain_body,
        grid=(m // tm, n // tn),
        in_specs=[_resident((tm, k), lambda i, j: (i, 0)),
                  _w_spec(k, tn, col0)],
        out_specs=pl.BlockSpec((tm, tn), lambda i, j: (i, j)),
        out_shape=jax.ShapeDtypeStruct((m, n), out_dtype),
        compiler_params=_params("arbitrary", "arbitrary"),
        name="mm_plain",
    )(a, w)


def _mm_gate_body(a_ref, w_ref, g_ref, b_ref, o_ref):
    gate = jax.nn.sigmoid(g_ref[...].astype(F32) + b_ref[...])
    o_ref[...] = (gate * _wdot(a_ref[...], w_ref)).astype(o_ref.dtype)


def _mm_gate_acc_body(a_ref, w_ref, g_ref, b_ref, p_ref, o_ref):
    gate = jax.nn.sigmoid(g_ref[...].astype(F32) + b_ref[...])
    o_ref[...] = (p_ref[...] + gate * _wdot(a_ref[...], w_ref)).astype(o_ref.dtype)


def _mm_gate(a, w, gpre, gcol0, bias, prev, out_dtype):
    m, k = a.shape
    n = w.shape[1]
    tm, tn = _mm_tiles(m, k, n, w.dtype.itemsize)
    joff = gcol0 // tn
    in_specs = [_resident((tm, k), lambda i, j: (i, 0)),
                pl.BlockSpec((k, tn), lambda i, j: (0, j)),
                pl.BlockSpec((tm, tn), lambda i, j: (i, j + joff)),
                pl.BlockSpec((1, tn), lambda i, j: (0, j))]
    args = [a, w, gpre, bias.reshape(1, n).astype(F32)]
    body = _mm_gate_body
    if prev is not None:
        in_specs.append(pl.BlockSpec((tm, tn), lambda i, j: (i, j)))
        args.append(prev)
        body = _mm_gate_acc_body
    return pl.pallas_call(
        body,
        grid=(m // tm, n // tn),
        in_specs=in_specs,
        out_specs=pl.BlockSpec((tm, tn), lambda i, j: (i, j)),
        out_shape=jax.ShapeDtypeStruct((m, n), out_dtype),
        compiler_params=_params("arbitrary", "arbitrary"),
        name="mm_gate",
    )(*args)


def _sconv_body(b_ref, c_ref, x_ref, w_ref, o_ref):
    u = c_ref[0].astype(F32) * x_ref[0].astype(F32)
    row = lax.broadcasted_iota(jnp.int32, u.shape, 0)
    w = w_ref[...]
    k_taps = w.shape[0]
    y = None
    for k in range(k_taps):
        shift = k_taps - 1 - k
        us = u if shift == 0 else jnp.where(row >= shift, pltpu.roll(u, shift, 0), 0.0)
        term = us * w[k:k + 1]
        y = term if y is None else y + term
    o_ref[0] = (b_ref[0].astype(F32) * y).astype(o_ref.dtype)


def _sconv(p3, w, width):
    bsz, length, _ = p3.shape
    tc = _tile(width, 256)
    nb = width // tc
    return pl.pallas_call(
        _sconv_body,
        grid=(bsz, nb),
        in_specs=[pl.BlockSpec((1, length, tc), lambda b, j: (b, 0, j)),
                  pl.BlockSpec((1, length, tc), lambda b, j: (b, 0, j + nb)),
                  pl.BlockSpec((1, length, tc), lambda b, j: (b, 0, j + 2 * nb)),
                  pl.BlockSpec((w.shape[0], tc), lambda b, j: (0, j))],
        out_specs=pl.BlockSpec((1, length, tc), lambda b, j: (b, 0, j)),
        out_shape=jax.ShapeDtypeStruct((bsz, length, width), BF16),
        compiler_params=_params("parallel", "parallel"),
        name="sconv",
    )(p3, p3, p3, w.astype(F32))


def _expand(v, e):
    hi = v.astype(BF16)
    r1 = v - hi.astype(F32)
    mid = r1.astype(BF16)
    lo = (r1 - mid.astype(F32)).astype(BF16)
    return _dot(hi, e) + _dot(mid, e) + _dot(lo, e)


def _ssd_body(x_ref, bm_ref, cm_ref, z_ref, dt_ref,
              wx_ref, wb_ref, wc_ref, bx_ref, bb_ref, bc_ref,
              dtb_ref, alog_ref, dx_ref, nw_ref, e1_ref, e2_ref,
              y_ref,
              state, halo_x, halo_b, halo_c, ext_x, ext_b, ext_c, dt_s, acs_s, acst_s, ybuf,
              *, heads_per_group):
    c = pl.program_id(1)
    g = pl.program_id(2)
    t = SSM_CHUNK
    r = heads_per_group
    rp = r * SSM_HEAD_DIM

    @pl.when(c == 0)
    def _():
        state[g] = jnp.zeros(state.shape[1:], F32)
        halo_x[g] = jnp.zeros(halo_x.shape[1:], F32)
        halo_b[g] = jnp.zeros(halo_b.shape[1:], F32)
        halo_c[g] = jnp.zeros(halo_c.shape[1:], F32)

    @pl.when(g == 0)
    def _():
        v = dt_ref[0] + dtb_ref[...]
        dt = jnp.maximum(v, 0.0) + jnp.log1p(jnp.exp(-jnp.abs(v)))
        a = dt * (-jnp.exp(alog_ref[...]))
        row = lax.broadcasted_iota(jnp.int32, a.shape, 0)
        acs = a
        s = 1
        while s < t:
            acs = acs + jnp.where(row >= s, pltpu.roll(acs, s, 0), 0.0)
            s *= 2
        dt_s[...] = dt
        acs_s[...] = acs
        acst = acs.T
        for gg in range(SSM_N_GROUPS):
            acst_s[gg] = acst[gg * r:(gg + 1) * r, :]

    dt = dt_s[...]
    acs = acs_s[...]
    eacs = jnp.exp(acs)
    dec = jnp.exp(acs[t - 1:t, :] - acs)
    e1 = e1_ref[...]
    dtx = _expand(dt, e1)
    decx = _expand(dec, e1)
    eacsx = _expand(eacs, e1)
    colb = _expand(acs, e2_ref[...])

    def conv(cur, halo, ext, w_ref, b_ref):
        ext[0:SUBLANES, :] = halo[g]
        ext[SUBLANES:SUBLANES + t, :] = cur
        w = w_ref[...]
        k_taps = w.shape[0]
        acc = None
        for k in range(k_taps):
            term = w[k:k + 1] * ext[pl.ds(SUBLANES - (k_taps - 1) + k, t), :]
            acc = term if acc is None else acc + term
        halo[g] = ext[t:t + SUBLANES, :]
        return _silu(acc + b_ref[...])

    xs = conv(x_ref[0].astype(F32), halo_x, ext_x, wx_ref, bx_ref)
    bmat = conv(bm_ref[0].astype(F32), halo_b, ext_b, wb_ref, bb_ref).astype(BF16)
    cmat = conv(cm_ref[0].astype(F32), halo_c, ext_c, wc_ref, bc_ref).astype(BF16)

    xdt = xs * dtx
    xdec_bf = (xdt * decx).astype(BF16)
    cb = lax.dot_general(cmat, bmat, (((1,), (1,)), ((), ())), preferred_element_type=F32)
    sprev = state[g]
    yoff = _dot(cmat, sprev.astype(BF16)) * eacsx
    snew = lax.dot_general(bmat, xdec_bf, (((0,), (0,)), ((), ())), preferred_element_type=F32)
    state[g] = eacsx[t - 1:t, :] * sprev + snew

    tri = (lax.broadcasted_iota(jnp.int32, (t, t), 0) >= lax.broadcasted_iota(jnp.int32, (t, t), 1))
    lane = lax.broadcasted_iota(jnp.int32, (t, LANES), 1)
    heads_per_tile = LANES // SSM_HEAD_DIM
    ss = jnp.zeros((t, 1), F32)
    for q in range(rp // LANES):
        sl = slice(q * LANES, (q + 1) * LANES)
        xdt_q = xdt[:, sl]
        acc = yoff[:, sl] + dx_ref[:, sl] * xs[:, sl]
        for e in range(heads_per_tile):
            k = q * heads_per_tile + e
            seg = colb[:, k * t:(k + 1) * t] - acst_s[g, pl.ds(k, 1), :]
            scores = (cb * jnp.exp(jnp.where(tri, seg, -jnp.inf))).astype(BF16)
            in_head = (lane >= e * SSM_HEAD_DIM) & (lane < (e + 1) * SSM_HEAD_DIM)
            acc = acc + _dot(scores, jnp.where(in_head, xdt_q, 0.0).astype(BF16))
        gated = acc * _silu(z_ref[0, :, sl].astype(F32))
        ybuf[:, sl] = gated
        ss = ss + jnp.sum(gated * gated, axis=-1, keepdims=True)
    inv = lax.rsqrt(ss / rp + RMS_EPS)
    y_ref[0] = (ybuf[...] * inv * nw_ref[...]).astype(y_ref.dtype)


def _ssd(xbc, z, dt_raw, conv_w, conv_b, dt_bias, a_log, d_skip, norm_w):
    bsz, length, d_inner = z.shape
    n_heads = dt_raw.shape[-1]
    g_, n_, t = SSM_N_GROUPS, SSM_D_STATE, SSM_CHUNK
    r = n_heads // g_
    rp = r * SSM_HEAD_DIM
    assert d_inner == n_heads * SSM_HEAD_DIM and rp % LANES == 0 and n_ == LANES and length % t == 0
    xb = d_inner // n_
    head_of_ch = jnp.arange(d_inner, dtype=jnp.int32) // SSM_HEAD_DIM
    e1 = (jnp.arange(n_heads, dtype=jnp.int32)[:, None] == head_of_ch[None, :]).astype(BF16)
    head_of_col = jnp.arange(n_heads * t, dtype=jnp.int32) // t
    e2 = (jnp.arange(n_heads, dtype=jnp.int32)[:, None] == head_of_col[None, :]).astype(BF16)
    d_x = jnp.repeat(d_skip.astype(F32), SSM_HEAD_DIM).reshape(1, d_inner)
    conv_w = conv_w.astype(F32)
    conv_b = conv_b.astype(F32).reshape(1, -1)
    kt = conv_w.shape[0]

    in_specs = [
        pl.BlockSpec((1, t, rp), lambda b, c, g: (b, c, g)),
        pl.BlockSpec((1, t, n_), lambda b, c, g: (b, c, xb + g)),
        pl.BlockSpec((1, t, n_), lambda b, c, g: (b, c, xb + g_ + g)),
        pl.BlockSpec((1, t, rp), lambda b, c, g: (b, c, g)),
        pl.BlockSpec((1, t, n_heads), lambda b, c, g: (b, c, 0)),
        pl.BlockSpec((kt, rp), lambda b, c, g: (0, g)),
        pl.BlockSpec((kt, n_), lambda b, c, g: (0, xb + g)),
        pl.BlockSpec((kt, n_), lambda b, c, g: (0, xb + g_ + g)),
        pl.BlockSpec((1, rp), lambda b, c, g: (0, g)),
        pl.BlockSpec((1, n_), lambda b, c, g: (0, xb + g)),
        pl.BlockSpec((1, n_), lambda b, c, g: (0, xb + g_ + g)),
        pl.BlockSpec((1, n_heads), lambda b, c, g: (0, 0)),
        pl.BlockSpec((1, n_heads), lambda b, c, g: (0, 0)),
        pl.BlockSpec((1, rp), lambda b, c, g: (0, g)),
        pl.BlockSpec((1, rp), lambda b, c, g: (0, g)),
        pl.BlockSpec((n_heads, rp), lambda b, c, g: (0, g)),
        pl.BlockSpec((n_heads, r * t), lambda b, c, g: (0, g)),
    ]
    scratch = [
        pltpu.VMEM((g_, n_, rp), F32),
        pltpu.VMEM((g_, SUBLANES, rp), F32),
        pltpu.VMEM((g_, SUBLANES, n_), F32),
        pltpu.VMEM((g_, SUBLANES, n_), F32),
        pltpu.VMEM((SUBLANES + t, rp), F32),
        pltpu.VMEM((SUBLANES + t, n_), F32),
        pltpu.VMEM((SUBLANES + t, n_), F32),
        pltpu.VMEM((t, n_heads), F32),
        pltpu.VMEM((t, n_heads), F32),
        pltpu.VMEM((g_, r, t), F32),
        pltpu.VMEM((t, rp), F32),
    ]
    return pl.pallas_call(
        functools.partial(_ssd_body, heads_per_group=r),
        grid=(bsz, length // t, g_),
        in_specs=in_specs,
        out_specs=pl.BlockSpec((1, t, rp), lambda b, c, g: (b, c, g)),
        out_shape=jax.ShapeDtypeStruct((bsz, length, d_inner), BF16),
        scratch_shapes=scratch,
        compiler_params=_params("arbitrary", "arbitrary", "arbitrary"),
        name="ssd",
    )(xbc, xbc, xbc, z, dt_raw, conv_w, conv_w, conv_w, conv_b, conv_b, conv_b,
      dt_bias.reshape(1, -1).astype(F32), a_log.reshape(1, -1).astype(F32), d_x,
      norm_w.reshape(1, -1).astype(F32), e1, e2)


def _swiglu_block(x2d, norm_w, w_gate, w_up, w_down):
    h = _rmsnorm(x2d, norm_w, BF16)
    act = _ffn_up(h, w_gate, w_up)
    return _mm_residual(act, w_down.astype(BF16), x2d, FFN_RESIDUAL_SCALE)


def _mixer_block(x2d, bsz, length, norm_w, w_in, gate_bias, sconv_w, sconv_w_out, ssm_conv_w, ssm_conv_b,
                 ssm_dt_bias, ssm_a_log, ssm_d, ssm_norm, ssm_w_out, w_o):
    d_model = x2d.shape[1]
    conv_width = sconv_w.shape[1]
    d_inner = ssm_norm.shape[0]
    conv_dim = ssm_conv_w.shape[1]
    n_heads = ssm_a_log.shape[0]
    o_z = 3 * conv_width
    o_xbc = o_z + d_inner
    o_dt = o_xbc + conv_dim
    o_g = o_dt + n_heads

    h = _rmsnorm(x2d, norm_w, BF16)
    p_conv = _mm_plain(h, w_in, 0, o_z, BF16)
    p_z = _mm_plain(h, w_in, o_z, d_inner, BF16)
    p_xbc = _mm_plain(h, w_in, o_xbc, conv_dim, BF16)
    p_dt = _mm_plain(h, w_in, o_dt, n_heads, F32)
    p_gate = _mm_plain(h, w_in, o_g, 2 * d_model, BF16)

    v = _sconv(p_conv.reshape(bsz, length, o_z), sconv_w, conv_width).reshape(bsz * length, conv_width)
    y = _ssd(p_xbc.reshape(bsz, length, conv_dim), p_z.reshape(bsz, length, d_inner),
             p_dt.reshape(bsz, length, n_heads), ssm_conv_w, ssm_conv_b, ssm_dt_bias, ssm_a_log, ssm_d,
             ssm_norm).reshape(bsz * length, d_inner)

    merged = _mm_gate(v, sconv_w_out, p_gate, 0, gate_bias[:d_model], None, F32)
    merged = _mm_gate(y, ssm_w_out, p_gate, d_model, gate_bias[d_model:], merged, BF16)
    return _mm_residual(merged, w_o, x2d, 1.0)


def kernel(x, ffn1_norm, ffn1_w_gate, ffn1_w_up, ffn1_w_down, mix_norm, w_in, gate_bias, sconv_w, sconv_w_out, ssm_conv_w, ssm_conv_b, ssm_dt_bias, ssm_A_log, ssm_D, ssm_norm, ssm_w_out, w_o, ffn2_norm, ffn2_w_gate, ffn2_w_up, ffn2_w_down, final_norm):
    bsz, length, d_model = x.shape
    x2d = x.reshape(bsz * length, d_model)
    for i in range(ffn1_norm.shape[0]):
        x2d = _swiglu_block(x2d, ffn1_norm[i], ffn1_w_gate[i], ffn1_w_up[i], ffn1_w_down[i])
        x2d = _mixer_block(x2d, bsz, length, mix_norm[i], w_in[i], gate_bias[i], sconv_w[i], sconv_w_out[i],
                           ssm_conv_w[i], ssm_conv_b[i], ssm_dt_bias[i], ssm_A_log[i], ssm_D[i], ssm_norm[i],
                           ssm_w_out[i], w_o[i])
        x2d = _swiglu_block(x2d, ffn2_norm[i], ffn2_w_gate[i], ffn2_w_up[i], ffn2_w_down[i])
    return _rmsnorm(x2d, final_norm, x.dtype).reshape(bsz, length, d_model)
```

```python
import functools
import math

import jax
import jax.numpy as jnp
from jax import lax
from jax.experimental import pallas as pl
from jax.experimental.pallas import tpu as pltpu

F32 = jnp.float32
BF16 = jnp.bfloat16

RMS_EPS = 1e-5
FFN_RESIDUAL_SCALE = 0.5
SSM_N_GROUPS = 8
SSM_D_STATE = 128
SSM_HEAD_DIM = 64
SSM_CHUNK = 128
SSM_GROUPS_PER_STEP = 4
LOG2_E = math.log2(math.e)
LANES = 128
SUBLANES = 8
VMEM_LIMIT_BYTES = 56 * 1024 * 1024


def _params(*sem):
    return pltpu.CompilerParams(dimension_semantics=sem, vmem_limit_bytes=VMEM_LIMIT_BYTES)


def _tile(n, pref):
    if n <= pref:
        return n
    t = (pref // LANES) * LANES
    while t >= LANES:
        if n % t == 0:
            return t
        t -= LANES
    return n


def _silu(v):
    return v * jax.nn.sigmoid(v)


def _dot(a, b):
    return jnp.dot(a, b, preferred_element_type=F32)


def _rmsnorm_body(x_ref, w_ref, o_ref):
    x = x_ref[...]
    ms = jnp.mean(x * x, axis=-1, keepdims=True)
    o_ref[...] = (x * lax.rsqrt(ms + RMS_EPS) * w_ref[...]).astype(o_ref.dtype)


def _rmsnorm(x2d, w, out_dtype):
    m, d = x2d.shape
    tm = _tile(m, 256)
    return pl.pallas_call(
        _rmsnorm_body,
        grid=(m // tm,),
        in_specs=[pl.BlockSpec((tm, d), lambda i: (i, 0)), pl.BlockSpec((1, d), lambda i: (0, 0))],
        out_specs=pl.BlockSpec((tm, d), lambda i: (i, 0)),
        out_shape=jax.ShapeDtypeStruct((m, d), out_dtype),
        compiler_params=_params("parallel"),
        name="rmsnorm",
    )(x2d, w.reshape(1, d).astype(F32))


def _resident(block, index_map):
    return pl.BlockSpec(block, index_map, pipeline_mode=pl.Buffered(1))


def _w_spec(k, tn, col0):
    if col0 % tn == 0:
        joff = col0 // tn
        return pl.BlockSpec((k, tn), lambda i, j: (0, j + joff))
    assert col0 % LANES == 0 and tn % LANES == 0
    return pl.BlockSpec((pl.Element(k), pl.Element(tn)),
                        lambda i, j: (0, pl.multiple_of(col0 + j * tn, LANES)))


def _wdot(a, w_ref):
    return _dot(a, w_ref[...].astype(BF16))


def _ffn_up_body(a_ref, wg_ref, wu_ref, o_ref):
    a = a_ref[...]
    g = _wdot(a, wg_ref)
    u = _wdot(a, wu_ref)
    o_ref[...] = (_silu(g) * u).astype(o_ref.dtype)


def _ffn_up(a, wg, wu):
    m, k = a.shape
    n = wg.shape[1]
    tm, tn = _tile(m, 2048), _tile(n, 256)
    return pl.pallas_call(
        _ffn_up_body,
        grid=(m // tm, n // tn),
        in_specs=[_resident((tm, k), lambda i, j: (i, 0)),
                  pl.BlockSpec((k, tn), lambda i, j: (0, j)),
                  pl.BlockSpec((k, tn), lambda i, j: (0, j))],
        out_specs=pl.BlockSpec((tm, tn), lambda i, j: (i, j)),
        out_shape=jax.ShapeDtypeStruct((m, n), BF16),
        compiler_params=_params("arbitrary", "arbitrary"),
        name="ffn_up",
    )(a, wg, wu)


def _mm_residual_body(a_ref, w_ref, r_ref, o_ref, *, scale):
    o_ref[...] = r_ref[...] + scale * _wdot(a_ref[...], w_ref)


def _mm_residual(a, w, res, scale, tm_pref, tn_pref):
    m, k = a.shape
    n = w.shape[1]
    tm, tn = _tile(m, tm_pref), _tile(n, tn_pref)
    return pl.pallas_call(
        functools.partial(_mm_residual_body, scale=scale),
        grid=(m // tm, n // tn),
        in_specs=[_resident((tm, k), lambda i, j: (i, 0)),
                  pl.BlockSpec((k, tn), lambda i, j: (0, j)),
                  pl.BlockSpec((tm, tn), lambda i, j: (i, j))],
        out_specs=pl.BlockSpec((tm, tn), lambda i, j: (i, j)),
        out_shape=jax.ShapeDtypeStruct((m, n), F32),
        compiler_params=_params("arbitrary", "arbitrary"),
        name="mm_residual",
    )(a, w, res)


def _mm_plain_body(a_ref, w_ref, o_ref, *, act):
    r = _wdot(a_ref[...], w_ref)
    o_ref[...] = (_silu(r) if act else r).astype(o_ref.dtype)


def _mm_plain(a, w, col0, n, out_dtype, act=False):
    m, k = a.shape
    tm, tn = _tile(m, 2048), _tile(n, 512)
    return pl.pallas_call(
        functools.partial(_mm_plain_body, act=act),
        grid=(m // tm, n // tn),
        in_specs=[_resident((tm, k), lambda i, j: (i, 0)),
                  _w_spec(k, tn, col0)],
        out_specs=pl.BlockSpec((tm, tn), lambda i, j: (i, j)),
        out_shape=jax.ShapeDtypeStruct((m, n), out_dtype),
        compiler_params=_params("arbitrary", "arbitrary"),
        name="mm_plain",
    )(a, w)


def _mm_conv_silu_body(a_ref, w_ref, cw_ref, cb_ref, o_ref):
    r = _wdot(a_ref[...], w_ref)
    taps = cw_ref[...]
    kt = taps.shape[0]
    row = lax.broadcasted_iota(jnp.int32, (SUBLANES, r.shape[1]), 0)
    acc = None
    for s in range(kt - 1, -1, -1):
        if s == 0:
            sh = r
        else:
            rolled = pltpu.roll(r, s, 0)
            head = jnp.where(row >= s, rolled[:SUBLANES], 0.0)
            sh = jnp.concatenate([head, rolled[SUBLANES:]], axis=0)
        term = sh * taps[kt - 1 - s:kt - s]
        acc = term if acc is None else acc + term
    o_ref[...] = _silu(acc + cb_ref[...]).astype(o_ref.dtype)


def _mm_conv_silu(a, w, col0, n, conv_w, conv_b, length):
    m, k = a.shape
    assert m % length == 0 and conv_w.shape[0] <= SUBLANES
    tm, tn = length, _tile(n, 256)
    return pl.pallas_call(
        _mm_conv_silu_body,
        grid=(m // tm, n // tn),
        in_specs=[_resident((tm, k), lambda i, j: (i, 0)),
                  _w_spec(k, tn, col0),
                  pl.BlockSpec((conv_w.shape[0], tn), lambda i, j: (0, j)),
                  pl.BlockSpec((1, tn), lambda i, j: (0, j))],
        out_specs=pl.BlockSpec((tm, tn), lambda i, j: (i, j)),
        out_shape=jax.ShapeDtypeStruct((m, n), BF16),
        compiler_params=_params("arbitrary", "arbitrary"),
        name="mm_conv_silu",
    )(a, w, conv_w.astype(F32), conv_b.reshape(1, n).astype(F32))


def _mm_gate_body(a_ref, w_ref, g_ref, b_ref, o_ref):
    gate = jax.nn.sigmoid(g_ref[...].astype(F32) + b_ref[...])
    o_ref[...] = (gate * _wdot(a_ref[...], w_ref)).astype(o_ref.dtype)


def _mm_gate_acc_body(a_ref, w_ref, g_ref, b_ref, p_ref, o_ref):
    gate = jax.nn.sigmoid(g_ref[...].astype(F32) + b_ref[...])
    o_ref[...] = (p_ref[...] + gate * _wdot(a_ref[...], w_ref)).astype(o_ref.dtype)


def _mm_gate(a, w, gpre, gcol0, bias, prev, out_dtype, tm_pref, tn_pref):
    m, k = a.shape
    n = w.shape[1]
    tm, tn = _tile(m, tm_pref), _tile(n, tn_pref)
    joff = gcol0 // tn
    in_specs = [_resident((tm, k), lambda i, j: (i, 0)),
                pl.BlockSpec((k, tn), lambda i, j: (0, j)),
                pl.BlockSpec((tm, tn), lambda i, j: (i, j + joff)),
                pl.BlockSpec((1, tn), lambda i, j: (0, j))]
    args = [a, w, gpre, bias.reshape(1, n).astype(F32)]
    body = _mm_gate_body
    if prev is not None:
        in_specs.append(pl.BlockSpec((tm, tn), lambda i, j: (i, j)))
        args.append(prev)
        body = _mm_gate_acc_body
    return pl.pallas_call(
        body,
        grid=(m // tm, n // tn),
        in_specs=in_specs,
        out_specs=pl.BlockSpec((tm, tn), lambda i, j: (i, j)),
        out_shape=jax.ShapeDtypeStruct((m, n), out_dtype),
        compiler_params=_params("arbitrary", "arbitrary"),
        name="mm_gate",
    )(*args)


def _sconv_body(b_ref, c_ref, x_ref, w_ref, o_ref):
    u = c_ref[0].astype(F32) * x_ref[0].astype(F32)
    row = lax.broadcasted_iota(jnp.int32, u.shape, 0)
    w = w_ref[...]
    k_taps = w.shape[0]
    y = None
    for k in range(k_taps):
        shift = k_taps - 1 - k
        us = u if shift == 0 else jnp.where(row >= shift, pltpu.roll(u, shift, 0), 0.0)
        term = us * w[k:k + 1]
        y = term if y is None else y + term
    o_ref[0] = (b_ref[0].astype(F32) * y).astype(o_ref.dtype)


def _sconv(p3, w, width):
    bsz, length, _ = p3.shape
    tc = _tile(width, 256)
    nb = width // tc
    return pl.pallas_call(
        _sconv_body,
        grid=(bsz, nb),
        in_specs=[pl.BlockSpec((1, length, tc), lambda b, j: (b, 0, j)),
                  pl.BlockSpec((1, length, tc), lambda b, j: (b, 0, j + nb)),
                  pl.BlockSpec((1, length, tc), lambda b, j: (b, 0, j + 2 * nb)),
                  pl.BlockSpec((w.shape[0], tc), lambda b, j: (0, j))],
        out_specs=pl.BlockSpec((1, length, tc), lambda b, j: (b, 0, j)),
        out_shape=jax.ShapeDtypeStruct((bsz, length, width), BF16),
        compiler_params=_params("parallel", "parallel"),
        name="sconv",
    )(p3, p3, p3, w.astype(F32))


def _split_bf16(v, parts):
    out = []
    rem = v
    for _ in range(parts):
        p = rem.astype(BF16)
        out.append(p)
        rem = rem - p.astype(F32)
    return jnp.concatenate(out, axis=1)


def _ssd_body(xs_ref, bm_ref, cm_ref, zs_ref, dt_ref, dtb_ref, alog_ref, dx_ref, nw_ref, e1_ref, e2_ref,
              y_ref, state, dtq_s, acs3_s, acst_s, *, heads_per_group, groups_per_step):
    c = pl.program_id(1)
    gi = pl.program_id(2)
    t = SSM_CHUNK
    n = SSM_D_STATE
    r = heads_per_group
    rp = r * SSM_HEAD_DIM
    gps = groups_per_step
    heads_per_tile = LANES // SSM_HEAD_DIM

    @pl.when(c == 0)
    def _():
        for s in range(gps):
            state[gi * gps + s] = jnp.zeros(state.shape[1:], F32)

    @pl.when(gi == 0)
    def _():
        v = dt_ref[0] + dtb_ref[...]
        dt = jnp.maximum(v, 0.0) + jnp.log1p(jnp.exp(-jnp.abs(v)))
        a = dt * (-jnp.exp(alog_ref[...]))
        row = lax.broadcasted_iota(jnp.int32, a.shape, 0)
        acs = a
        s = 1
        while s < t:
            acs = acs + jnp.where(row >= s, pltpu.roll(acs, s, 0), 0.0)
            s *= 2
        eacs = jnp.exp(acs)
        dec = jnp.exp(acs[t - 1:t, :] - acs)
        dtq_s[...] = _split_bf16(jnp.concatenate([dt, dec, eacs], axis=0), 2)
        acs2 = acs * LOG2_E
        acs3_s[...] = _split_bf16(acs2, 3)
        acst = acs2.T
        for gg in range(SSM_N_GROUPS):
            acst_s[gg] = acst[gg * r:(gg + 1) * r, :]

    tri = (lax.broadcasted_iota(jnp.int32, (t, t), 0) >= lax.broadcasted_iota(jnp.int32, (t, t), 1))
    lane = lax.broadcasted_iota(jnp.int32, (t, LANES), 1)
    dtq = dtq_s[...]
    acs3 = acs3_s[...]

    for s in range(gps):
        g = gi * gps + s
        ch = slice(s * rp, (s + 1) * rp)
        ex = _dot(dtq, e1_ref[:, ch])
        dtx, decx, eacsx = ex[0:t], ex[t:2 * t], ex[2 * t:3 * t]
        colb = _dot(acs3, e2_ref[:, s * r * t:(s + 1) * r * t])
        xs = xs_ref[0, :, ch].astype(F32)
        bmat = bm_ref[0, :, s * n:(s + 1) * n]
        cmat = cm_ref[0, :, s * n:(s + 1) * n]
        xdt = xs * dtx
        xdec_bf = (xdt * decx).astype(BF16)
        cb = lax.dot_general(cmat, bmat, (((1,), (1,)), ((), ())), preferred_element_type=F32)
        sprev = state[g]
        yoff = _dot(cmat, sprev.astype(BF16)) * eacsx
        snew = lax.dot_general(bmat, xdec_bf, (((0,), (0,)), ((), ())), preferred_element_type=F32)
        state[g] = eacsx[t - 1:t, :] * sprev + snew

        tiles = []
        ss = jnp.zeros((t, 1), F32)
        for q in range(rp // LANES):
            sl = slice(q * LANES, (q + 1) * LANES)
            xdt_q = xdt[:, sl]
            scores, xparts = [], []
            for e in range(heads_per_tile):
                k = q * heads_per_tile + e
                seg = colb[:, k * t:(k + 1) * t] - acst_s[g, pl.ds(k, 1), :]
                scores.append((cb * jnp.exp2(jnp.where(tri, seg, -jnp.inf))).astype(BF16))
                in_head = (lane >= e * SSM_HEAD_DIM) & (lane < (e + 1) * SSM_HEAD_DIM)
                xparts.append(jnp.where(in_head, xdt_q, 0.0).astype(BF16))
            ydiag = _dot(jnp.concatenate(scores, axis=1), jnp.concatenate(xparts, axis=0))
            acc = ydiag + yoff[:, sl] + dx_ref[:, s * rp + q * LANES:s * rp + (q + 1) * LANES] * xs[:, sl]
            gated = acc * zs_ref[0, :, s * rp + q * LANES:s * rp + (q + 1) * LANES].astype(F32)
            tiles.append(gated)
            ss = ss + jnp.sum(gated * gated, axis=-1, keepdims=True)
        inv = lax.rsqrt(ss / rp + RMS_EPS)
        for q, gated in enumerate(tiles):
            cs = slice(s * rp + q * LANES, s * rp + (q + 1) * LANES)
            y_ref[0, :, cs] = (gated * inv * nw_ref[:, cs]).astype(y_ref.dtype)


def _ssd(xbc, zs, dt_raw, dt_bias, a_log, d_skip, norm_w):
    bsz, length, d_inner = zs.shape
    n_heads = dt_raw.shape[-1]
    g_, n_, t = SSM_N_GROUPS, SSM_D_STATE, SSM_CHUNK
    r = n_heads // g_
    rp = r * SSM_HEAD_DIM
    gps = SSM_GROUPS_PER_STEP
    assert d_inner == n_heads * SSM_HEAD_DIM and rp % LANES == 0 and n_ == LANES and length % t == 0
    assert g_ % gps == 0
    wx, wn = gps * rp, gps * n_
    xb = d_inner // wn
    head_of_ch = jnp.arange(d_inner, dtype=jnp.int32) // SSM_HEAD_DIM
    e1 = (jnp.arange(n_heads, dtype=jnp.int32)[:, None] == head_of_ch[None, :]).astype(BF16)
    head_of_col = jnp.arange(n_heads * t, dtype=jnp.int32) // t
    e2 = (jnp.arange(n_heads, dtype=jnp.int32)[:, None] == head_of_col[None, :]).astype(BF16)
    e1 = jnp.tile(e1, (2, 1))
    e2 = jnp.tile(e2, (3, 1))
    d_x = jnp.repeat(d_skip.astype(F32), SSM_HEAD_DIM).reshape(1, d_inner)

    in_specs = [
        pl.BlockSpec((1, t, wx), lambda b, c, g: (b, c, g)),
        pl.BlockSpec((1, t, wn), lambda b, c, g: (b, c, xb + g)),
        pl.BlockSpec((1, t, wn), lambda b, c, g: (b, c, xb + g_ // gps + g)),
        pl.BlockSpec((1, t, wx), lambda b, c, g: (b, c, g)),
        pl.BlockSpec((1, t, n_heads), lambda b, c, g: (b, c, 0)),
        pl.BlockSpec((1, n_heads), lambda b, c, g: (0, 0)),
        pl.BlockSpec((1, n_heads), lambda b, c, g: (0, 0)),
        pl.BlockSpec((1, wx), lambda b, c, g: (0, g)),
        pl.BlockSpec((1, wx), lambda b, c, g: (0, g)),
        pl.BlockSpec((2 * n_heads, wx), lambda b, c, g: (0, g)),
        pl.BlockSpec((3 * n_heads, gps * r * t), lambda b, c, g: (0, g)),
    ]
    scratch = [
        pltpu.VMEM((g_, n_, rp), F32),
        pltpu.VMEM((3 * t, 2 * n_heads), BF16),
        pltpu.VMEM((t, 3 * n_heads), BF16),
        pltpu.VMEM((g_, r, t), F32),
    ]
    return pl.pallas_call(
        functools.partial(_ssd_body, heads_per_group=r, groups_per_step=gps),
        grid=(bsz, length // t, g_ // gps),
        in_specs=in_specs,
        out_specs=pl.BlockSpec((1, t, wx), lambda b, c, g: (b, c, g)),
        out_shape=jax.ShapeDtypeStruct((bsz, length, d_inner), BF16),
        scratch_shapes=scratch,
        compiler_params=_params("arbitrary", "arbitrary", "arbitrary"),
        name="ssd",
    )(xbc, xbc, xbc, zs, dt_raw,
      dt_bias.reshape(1, -1).astype(F32), a_log.reshape(1, -1).astype(F32), d_x,
      norm_w.reshape(1, -1).astype(F32), e1, e2)


def _swiglu_block(x2d, norm_w, w_gate, w_up, w_down):
    h = _rmsnorm(x2d, norm_w, BF16)
    act = _ffn_up(h, w_gate, w_up)
    return _mm_residual(act, w_down.astype(BF16), x2d, FFN_RESIDUAL_SCALE, 1024, 256)


def _mixer_block(x2d, bsz, length, norm_w, w_in, gate_bias, sconv_w, sconv_w_out, ssm_conv_w, ssm_conv_b,
                 ssm_dt_bias, ssm_a_log, ssm_d, ssm_norm, ssm_w_out, w_o):
    d_model = x2d.shape[1]
    conv_width = sconv_w.shape[1]
    d_inner = ssm_norm.shape[0]
    conv_dim = ssm_conv_w.shape[1]
    n_heads = ssm_a_log.shape[0]
    o_z = 3 * conv_width
    o_xbc = o_z + d_inner
    o_dt = o_xbc + conv_dim
    o_g = o_dt + n_heads

    h = _rmsnorm(x2d, norm_w, BF16)
    p_conv = _mm_plain(h, w_in, 0, o_z, BF16)
    zs = _mm_plain(h, w_in, o_z, d_inner, BF16, act=True)
    xbc = _mm_conv_silu(h, w_in, o_xbc, conv_dim, ssm_conv_w, ssm_conv_b, length)
    p_dt = _mm_plain(h, w_in, o_dt, n_heads, F32)
    p_gate = _mm_plain(h, w_in, o_g, 2 * d_model, BF16)

    v = _sconv(p_conv.reshape(bsz, length, o_z), sconv_w, conv_width).reshape(bsz * length, conv_width)
    y = _ssd(xbc.reshape(bsz, length, conv_dim), zs.reshape(bsz, length, d_inner),
             p_dt.reshape(bsz, length, n_heads), ssm_dt_bias, ssm_a_log, ssm_d,
             ssm_norm).reshape(bsz * length, d_inner)

    merged = _mm_gate(v, sconv_w_out, p_gate, 0, gate_bias[:d_model], None, F32, 2048, 256)
    merged = _mm_gate(y, ssm_w_out, p_gate, d_model, gate_bias[d_model:], merged, BF16, 1024, 256)
    return _mm_residual(merged, w_o, x2d, 1.0, 2048, 256)


def kernel(x, ffn1_norm, ffn1_w_gate, ffn1_w_up, ffn1_w_down, mix_norm, w_in, gate_bias, sconv_w, sconv_w_out, ssm_conv_w, ssm_conv_b, ssm_dt_bias, ssm_A_log, ssm_D, ssm_norm, ssm_w_out, w_o, ffn2_norm, ffn2_w_gate, ffn2_w_up, ffn2_w_down, final_norm):
    bsz, length, d_model = x.shape
    x2d = x.reshape(bsz * length, d_model)
    for i in range(ffn1_norm.shape[0]):
        x2d = _swiglu_block(x2d, ffn1_norm[i], ffn1_w_gate[i], ffn1_w_up[i], ffn1_w_down[i])
        x2d = _mixer_block(x2d, bsz, length, mix_norm[i], w_in[i], gate_bias[i], sconv_w[i], sconv_w_out[i],
                           ssm_conv_w[i], ssm_conv_b[i], ssm_dt_bias[i], ssm_A_log[i], ssm_D[i], ssm_norm[i],
                           ssm_w_out[i], w_o[i])
        x2d = _swiglu_block(x2d, ffn2_norm[i], ffn2_w_gate[i], ffn2_w_up[i], ffn2_w_down[i])
    return _rmsnorm(x2d, final_norm, x.dtype).reshape(bsz, length, d_model)
```

```python
import functools
import math

import jax
import jax.numpy as jnp
from jax import lax
from jax.experimental import pallas as pl
from jax.experimental.pallas import tpu as pltpu

F32 = jnp.float32
BF16 = jnp.bfloat16

RMS_EPS = 1e-5
FFN_RESIDUAL_SCALE = 0.5
SSM_N_GROUPS = 8
SSM_D_STATE = 128
SSM_HEAD_DIM = 64
SSM_CHUNK = 128
SSM_GROUPS_PER_STEP = 4
LOG2_E = math.log2(math.e)
LANES = 128
SUBLANES = 8
VMEM_LIMIT_BYTES = 56 * 1024 * 1024
MM_ROW_CHUNK = 512
CONV_ROW_BLOCK = 32


def _params(*sem):
    return pltpu.CompilerParams(dimension_semantics=sem, vmem_limit_bytes=VMEM_LIMIT_BYTES)


def _tile(n, pref):
    if n <= pref:
        return n
    t = (pref // LANES) * LANES
    while t >= LANES:
        if n % t == 0:
            return t
        t -= LANES
    return n


def _silu(v):
    return v * jax.nn.sigmoid(v)


def _dot(a, b):
    return jnp.dot(a, b, preferred_element_type=F32)


def _rmsnorm_body(x_ref, w_ref, o_ref):
    x = x_ref[...]
    ms = jnp.mean(x * x, axis=-1, keepdims=True)
    o_ref[...] = (x * lax.rsqrt(ms + RMS_EPS) * w_ref[...]).astype(o_ref.dtype)


def _rmsnorm(x2d, w, out_dtype):
    m, d = x2d.shape
    tm = _tile(m, 256)
    return pl.pallas_call(
        _rmsnorm_body,
        grid=(m // tm,),
        in_specs=[pl.BlockSpec((tm, d), lambda i: (i, 0)), pl.BlockSpec((1, d), lambda i: (0, 0))],
        out_specs=pl.BlockSpec((tm, d), lambda i: (i, 0)),
        out_shape=jax.ShapeDtypeStruct((m, d), out_dtype),
        compiler_params=_params("parallel"),
        name="rmsnorm",
    )(x2d, w.reshape(1, d).astype(F32))


def _resident(block, index_map):
    return pl.BlockSpec(block, index_map, pipeline_mode=pl.Buffered(1))


def _w_spec(k, tn, col0):
    if col0 % tn == 0:
        joff = col0 // tn
        return pl.BlockSpec((k, tn), lambda i, j: (0, j + joff))
    assert col0 % LANES == 0 and tn % LANES == 0
    return pl.BlockSpec((pl.Element(k), pl.Element(tn)),
                        lambda i, j: (0, pl.multiple_of(col0 + j * tn, LANES)))


def _row_chunks(rows):
    step = min(rows, MM_ROW_CHUNK)
    assert rows % step == 0
    return [slice(r0, r0 + step) for r0 in range(0, rows, step)]


def _ffn_up_body(a_ref, wg_ref, wu_ref, o_ref):
    wg = wg_ref[...].astype(BF16)
    wu = wu_ref[...].astype(BF16)
    for rows in _row_chunks(a_ref.shape[0]):
        a = a_ref[rows, :]
        g = _dot(a, wg)
        u = _dot(a, wu)
        o_ref[rows, :] = (_silu(g) * u).astype(o_ref.dtype)


def _ffn_up(a, wg, wu):
    m, k = a.shape
    n = wg.shape[1]
    tm, tn = _tile(m, 2048), _tile(n, 256)
    return pl.pallas_call(
        _ffn_up_body,
        grid=(m // tm, n // tn),
        in_specs=[_resident((tm, k), lambda i, j: (i, 0)),
                  pl.BlockSpec((k, tn), lambda i, j: (0, j)),
                  pl.BlockSpec((k, tn), lambda i, j: (0, j))],
        out_specs=pl.BlockSpec((tm, tn), lambda i, j: (i, j)),
        out_shape=jax.ShapeDtypeStruct((m, n), BF16),
        compiler_params=_params("arbitrary", "arbitrary"),
        name="ffn_up",
    )(a, wg, wu)


def _mm_residual_body(a_ref, w_ref, r_ref, o_ref, *, scale):
    w = w_ref[...].astype(BF16)
    for rows in _row_chunks(a_ref.shape[0]):
        o_ref[rows, :] = r_ref[rows, :] + scale * _dot(a_ref[rows, :], w)


def _mm_residual(a, w, res, scale, tm_pref, tn_pref):
    m, k = a.shape
    n = w.shape[1]
    tm, tn = _tile(m, tm_pref), _tile(n, tn_pref)
    return pl.pallas_call(
        functools.partial(_mm_residual_body, scale=scale),
        grid=(m // tm, n // tn),
        in_specs=[_resident((tm, k), lambda i, j: (i, 0)),
                  pl.BlockSpec((k, tn), lambda i, j: (0, j)),
                  pl.BlockSpec((tm, tn), lambda i, j: (i, j))],
        out_specs=pl.BlockSpec((tm, tn), lambda i, j: (i, j)),
        out_shape=jax.ShapeDtypeStruct((m, n), F32),
        compiler_params=_params("arbitrary", "arbitrary"),
        name="mm_residual",
    )(a, w, res)


def _mm_plain_body(a_ref, w_ref, o_ref, *, act):
    w = w_ref[...].astype(BF16)
    for rows in _row_chunks(a_ref.shape[0]):
        r = _dot(a_ref[rows, :], w)
        o_ref[rows, :] = (_silu(r) if act else r).astype(o_ref.dtype)


def _mm_plain(a, w, col0, n, out_dtype, act=False):
    m, k = a.shape
    tm, tn = _tile(m, 2048), _tile(n, 512)
    return pl.pallas_call(
        functools.partial(_mm_plain_body, act=act),
        grid=(m // tm, n // tn),
        in_specs=[_resident((tm, k), lambda i, j: (i, 0)),
                  _w_spec(k, tn, col0)],
        out_specs=pl.BlockSpec((tm, tn), lambda i, j: (i, j)),
        out_shape=jax.ShapeDtypeStruct((m, n), out_dtype),
        compiler_params=_params("arbitrary", "arbitrary"),
        name="mm_plain",
    )(a, w)


def _mm_conv_silu_body(a_ref, w_ref, cw_ref, cb_ref, o_ref, raw):
    w = w_ref[...].astype(BF16)
    taps = cw_ref[...]
    bias = cb_ref[...]
    kt = taps.shape[0]
    raw[0:SUBLANES, :] = jnp.zeros((SUBLANES, w.shape[1]), F32)
    for rows in _row_chunks(a_ref.shape[0]):
        raw[rows.start + SUBLANES:rows.stop + SUBLANES, :] = _dot(a_ref[rows, :], w)
        for r0 in range(rows.start, rows.stop, CONV_ROW_BLOCK):
            ext = raw[r0:r0 + CONV_ROW_BLOCK + SUBLANES, :]
            acc = ext[SUBLANES:] * taps[kt - 1:kt]
            for s in range(1, kt):
                acc = acc + pltpu.roll(ext, s, 0)[SUBLANES:] * taps[kt - 1 - s:kt - s]
            o_ref[r0:r0 + CONV_ROW_BLOCK, :] = _silu(acc + bias).astype(o_ref.dtype)


def _mm_conv_silu(a, w, col0, n, conv_w, conv_b, length):
    m, k = a.shape
    assert m % length == 0 and conv_w.shape[0] <= SUBLANES and min(length, MM_ROW_CHUNK) % CONV_ROW_BLOCK == 0
    tm, tn = length, _tile(n, 256)
    return pl.pallas_call(
        _mm_conv_silu_body,
        grid=(m // tm, n // tn),
        in_specs=[_resident((tm, k), lambda i, j: (i, 0)),
                  _w_spec(k, tn, col0),
                  pl.BlockSpec((conv_w.shape[0], tn), lambda i, j: (0, j)),
                  pl.BlockSpec((1, tn), lambda i, j: (0, j))],
        out_specs=pl.BlockSpec((tm, tn), lambda i, j: (i, j)),
        out_shape=jax.ShapeDtypeStruct((m, n), BF16),
        scratch_shapes=[pltpu.VMEM((tm + SUBLANES, tn), F32)],
        compiler_params=_params("arbitrary", "arbitrary"),
        name="mm_conv_silu",
    )(a, w, conv_w.astype(F32), conv_b.reshape(1, n).astype(F32))


def _mm_gate_body(a_ref, w_ref, g_ref, b_ref, o_ref):
    w = w_ref[...].astype(BF16)
    bias = b_ref[...]
    for rows in _row_chunks(a_ref.shape[0]):
        gate = jax.nn.sigmoid(g_ref[rows, :].astype(F32) + bias)
        o_ref[rows, :] = (gate * _dot(a_ref[rows, :], w)).astype(o_ref.dtype)


def _mm_gate_acc_body(a_ref, w_ref, g_ref, b_ref, p_ref, o_ref):
    w = w_ref[...].astype(BF16)
    bias = b_ref[...]
    for rows in _row_chunks(a_ref.shape[0]):
        gate = jax.nn.sigmoid(g_ref[rows, :].astype(F32) + bias)
        o_ref[rows, :] = (p_ref[rows, :] + gate * _dot(a_ref[rows, :], w)).astype(o_ref.dtype)


def _mm_gate(a, w, gpre, gcol0, bias, prev, out_dtype, tm_pref, tn_pref):
    m, k = a.shape
    n = w.shape[1]
    tm, tn = _tile(m, tm_pref), _tile(n, tn_pref)
    joff = gcol0 // tn
    in_specs = [_resident((tm, k), lambda i, j: (i, 0)),
                pl.BlockSpec((k, tn), lambda i, j: (0, j)),
                pl.BlockSpec((tm, tn), lambda i, j: (i, j + joff)),
                pl.BlockSpec((1, tn), lambda i, j: (0, j))]
    args = [a, w, gpre, bias.reshape(1, n).astype(F32)]
    body = _mm_gate_body
    if prev is not None:
        in_specs.append(pl.BlockSpec((tm, tn), lambda i, j: (i, j)))
        args.append(prev)
        body = _mm_gate_acc_body
    return pl.pallas_call(
        body,
        grid=(m // tm, n // tn),
        in_specs=in_specs,
        out_specs=pl.BlockSpec((tm, tn), lambda i, j: (i, j)),
        out_shape=jax.ShapeDtypeStruct((m, n), out_dtype),
        compiler_params=_params("arbitrary", "arbitrary"),
        name="mm_gate",
    )(*args)


def _sconv_body(b_ref, c_ref, x_ref, w_ref, o_ref):
    u = c_ref[0].astype(F32) * x_ref[0].astype(F32)
    row = lax.broadcasted_iota(jnp.int32, u.shape, 0)
    w = w_ref[...]
    k_taps = w.shape[0]
    y = None
    for k in range(k_taps):
        shift = k_taps - 1 - k
        us = u if shift == 0 else jnp.where(row >= shift, pltpu.roll(u, shift, 0), 0.0)
        term = us * w[k:k + 1]
        y = term if y is None else y + term
    o_ref[0] = (b_ref[0].astype(F32) * y).astype(o_ref.dtype)


def _sconv(p3, w, width):
    bsz, length, _ = p3.shape
    tc = _tile(width, 256)
    nb = width // tc
    return pl.pallas_call(
        _sconv_body,
        grid=(bsz, nb),
        in_specs=[pl.BlockSpec((1, length, tc), lambda b, j: (b, 0, j)),
                  pl.BlockSpec((1, length, tc), lambda b, j: (b, 0, j + nb)),
                  pl.BlockSpec((1, length, tc), lambda b, j: (b, 0, j + 2 * nb)),
                  pl.BlockSpec((w.shape[0], tc), lambda b, j: (0, j))],
        out_specs=pl.BlockSpec((1, length, tc), lambda b, j: (b, 0, j)),
        out_shape=jax.ShapeDtypeStruct((bsz, length, width), BF16),
        compiler_params=_params("parallel", "parallel"),
        name="sconv",
    )(p3, p3, p3, w.astype(F32))


def _split_bf16(v, parts):
    out = []
    rem = v
    for _ in range(parts):
        p = rem.astype(BF16)
        out.append(p)
        rem = rem - p.astype(F32)
    return jnp.concatenate(out, axis=1)


def _ssd_body(xs_ref, bm_ref, cm_ref, zs_ref, dt_ref, dtb_ref, alog_ref, dx_ref, nw_ref, e1_ref, e2_ref,
              y_ref, state, dtq_s, acs3_s, acst_s, *, heads_per_group, groups_per_step):
    c = pl.program_id(1)
    gi = pl.program_id(2)
    t = SSM_CHUNK
    n = SSM_D_STATE
    r = heads_per_group
    rp = r * SSM_HEAD_DIM
    gps = groups_per_step
    heads_per_tile = LANES // SSM_HEAD_DIM

    @pl.when(c == 0)
    def _():
        for s in range(gps):
            state[gi * gps + s] = jnp.zeros(state.shape[1:], F32)

    @pl.when(gi == 0)
    def _():
        v = dt_ref[0] + dtb_ref[...]
        dt = jnp.maximum(v, 0.0) + jnp.log1p(jnp.exp(-jnp.abs(v)))
        a = dt * (-jnp.exp(alog_ref[...]))
        row = lax.broadcasted_iota(jnp.int32, a.shape, 0)
        acs = a
        s = 1
        while s < t:
            acs = acs + jnp.where(row >= s, pltpu.roll(acs, s, 0), 0.0)
            s *= 2
        eacs = jnp.exp(acs)
        dec = jnp.exp(acs[t - 1:t, :] - acs)
        dtq_s[...] = _split_bf16(jnp.concatenate([dt, dec, eacs], axis=0), 2)
        acs2 = acs * LOG2_E
        acs3_s[...] = _split_bf16(acs2, 3)
        acst = acs2.T
        for gg in range(SSM_N_GROUPS):
            acst_s[gg] = acst[gg * r:(gg + 1) * r, :]

    tri = (lax.broadcasted_iota(jnp.int32, (t, t), 0) >= lax.broadcasted_iota(jnp.int32, (t, t), 1))
    lane = lax.broadcasted_iota(jnp.int32, (t, LANES), 1)
    dtq = dtq_s[...]
    acs3 = acs3_s[...]

    for s in range(gps):
        g = gi * gps + s
        ch = slice(s * rp, (s + 1) * rp)
        ex = _dot(dtq, e1_ref[:, ch])
        dtx, decx, eacsx = ex[0:t], ex[t:2 * t], ex[2 * t:3 * t]
        colb = _dot(acs3, e2_ref[:, s * r * t:(s + 1) * r * t])
        xs = xs_ref[0, :, ch].astype(F32)
        bmat = bm_ref[0, :, s * n:(s + 1) * n]
        cmat = cm_ref[0, :, s * n:(s + 1) * n]
        xdt = xs * dtx
        xdec_bf = (xdt * decx).astype(BF16)
        cb = lax.dot_general(cmat, bmat, (((1,), (1,)), ((), ())), preferred_element_type=F32)
        sprev = state[g]
        yoff = _dot(cmat, sprev.astype(BF16)) * eacsx
        snew = lax.dot_general(bmat, xdec_bf, (((0,), (0,)), ((), ())), preferred_element_type=F32)
        state[g] = eacsx[t - 1:t, :] * sprev + snew

        tiles = []
        ss = jnp.zeros((t, 1), F32)
        for q in range(rp // LANES):
            sl = slice(q * LANES, (q + 1) * LANES)
            xdt_q = xdt[:, sl]
            scores, xparts = [], []
            for e in range(heads_per_tile):
                k = q * heads_per_tile + e
                seg = colb[:, k * t:(k + 1) * t] - acst_s[g, pl.ds(k, 1), :]
                scores.append((cb * jnp.exp2(jnp.where(tri, seg, -jnp.inf))).astype(BF16))
                in_head = (lane >= e * SSM_HEAD_DIM) & (lane < (e + 1) * SSM_HEAD_DIM)
                xparts.append(jnp.where(in_head, xdt_q, 0.0).astype(BF16))
            ydiag = _dot(jnp.concatenate(scores, axis=1), jnp.concatenate(xparts, axis=0))
            acc = ydiag + yoff[:, sl] + dx_ref[:, s * rp + q * LANES:s * rp + (q + 1) * LANES] * xs[:, sl]
            gated = acc * zs_ref[0, :, s * rp + q * LANES:s * rp + (q + 1) * LANES].astype(F32)
            tiles.append(gated)
            ss = ss + jnp.sum(gated * gated, axis=-1, keepdims=True)
        inv = lax.rsqrt(ss / rp + RMS_EPS)
        for q, gated in enumerate(tiles):
            cs = slice(s * rp + q * LANES, s * rp + (q + 1) * LANES)
            y_ref[0, :, cs] = (gated * inv * nw_ref[:, cs]).astype(y_ref.dtype)


def _ssd(xbc, zs, dt_raw, dt_bias, a_log, d_skip, norm_w):
    bsz, length, d_inner = zs.shape
    n_heads = dt_raw.shape[-1]
    g_, n_, t = SSM_N_GROUPS, SSM_D_STATE, SSM_CHUNK
    r = n_heads // g_
    rp = r * SSM_HEAD_DIM
    gps = SSM_GROUPS_PER_STEP
    assert d_inner == n_heads * SSM_HEAD_DIM and rp % LANES == 0 and n_ == LANES and length % t == 0
    assert g_ % gps == 0
    wx, wn = gps * rp, gps * n_
    xb = d_inner // wn
    head_of_ch = jnp.arange(d_inner, dtype=jnp.int32) // SSM_HEAD_DIM
    e1 = (jnp.arange(n_heads, dtype=jnp.int32)[:, None] == head_of_ch[None, :]).astype(BF16)
    head_of_col = jnp.arange(n_heads * t, dtype=jnp.int32) // t
    e2 = (jnp.arange(n_heads, dtype=jnp.int32)[:, None] == head_of_col[None, :]).astype(BF16)
    e1 = jnp.tile(e1, (2, 1))
    e2 = jnp.tile(e2, (3, 1))
    d_x = jnp.repeat(d_skip.astype(F32), SSM_HEAD_DIM).reshape(1, d_inner)

    in_specs = [
        pl.BlockSpec((1, t, wx), lambda b, c, g: (b, c, g)),
        pl.BlockSpec((1, t, wn), lambda b, c, g: (b, c, xb + g)),
        pl.BlockSpec((1, t, wn), lambda b, c, g: (b, c, xb + g_ // gps + g)),
        pl.BlockSpec((1, t, wx), lambda b, c, g: (b, c, g)),
        pl.BlockSpec((1, t, n_heads), lambda b, c, g: (b, c, 0)),
        pl.BlockSpec((1, n_heads), lambda b, c, g: (0, 0)),
        pl.BlockSpec((1, n_heads), lambda b, c, g: (0, 0)),
        pl.BlockSpec((1, wx), lambda b, c, g: (0, g)),
        pl.BlockSpec((1, wx), lambda b, c, g: (0, g)),
        pl.BlockSpec((2 * n_heads, wx), lambda b, c, g: (0, g)),
        pl.BlockSpec((3 * n_heads, gps * r * t), lambda b, c, g: (0, g)),
    ]
    scratch = [
        pltpu.VMEM((g_, n_, rp), F32),
        pltpu.VMEM((3 * t, 2 * n_heads), BF16),
        pltpu.VMEM((t, 3 * n_heads), BF16),
        pltpu.VMEM((g_, r, t), F32),
    ]
    return pl.pallas_call(
        functools.partial(_ssd_body, heads_per_group=r, groups_per_step=gps),
        grid=(bsz, length // t, g_ // gps),
        in_specs=in_specs,
        out_specs=pl.BlockSpec((1, t, wx), lambda b, c, g: (b, c, g)),
        out_shape=jax.ShapeDtypeStruct((bsz, length, d_inner), BF16),
        scratch_shapes=scratch,
        compiler_params=_params("arbitrary", "arbitrary", "arbitrary"),
        name="ssd",
    )(xbc, xbc, xbc, zs, dt_raw,
      dt_bias.reshape(1, -1).astype(F32), a_log.reshape(1, -1).astype(F32), d_x,
      norm_w.reshape(1, -1).astype(F32), e1, e2)


def _swiglu_block(x2d, norm_w, w_gate, w_up, w_down):
    h = _rmsnorm(x2d, norm_w, BF16)
    act = _ffn_up(h, w_gate, w_up)
    return _mm_residual(act, w_down.astype(BF16), x2d, FFN_RESIDUAL_SCALE, 1024, 256)


def _mixer_block(x2d, bsz, length, norm_w, w_in, gate_bias, sconv_w, sconv_w_out, ssm_conv_w, ssm_conv_b,
                 ssm_dt_bias, ssm_a_log, ssm_d, ssm_norm, ssm_w_out, w_o):
    d_model = x2d.shape[1]
    conv_width = sconv_w.shape[1]
    d_inner = ssm_norm.shape[0]
    conv_dim = ssm_conv_w.shape[1]
    n_heads = ssm_a_log.shape[0]
    o_z = 3 * conv_width
    o_xbc = o_z + d_inner
    o_dt = o_xbc + conv_dim
    o_g = o_dt + n_heads

    h = _rmsnorm(x2d, norm_w, BF16)
    p_conv = _mm_plain(h, w_in, 0, o_z, BF16)
    zs = _mm_plain(h, w_in, o_z, d_inner, BF16, act=True)
    xbc = _mm_conv_silu(h, w_in, o_xbc, conv_dim, ssm_conv_w, ssm_conv_b, length)
    p_dt = _mm_plain(h, w_in, o_dt, n_heads, F32)
    p_gate = _mm_plain(h, w_in, o_g, 2 * d_model, BF16)

    v = _sconv(p_conv.reshape(bsz, length, o_z), sconv_w, conv_width).reshape(bsz * length, conv_width)
    y = _ssd(xbc.reshape(bsz, length, conv_dim), zs.reshape(bsz, length, d_inner),
             p_dt.reshape(bsz, length, n_heads), ssm_dt_bias, ssm_a_log, ssm_d,
             ssm_norm).reshape(bsz * length, d_inner)

    merged = _mm_gate(v, sconv_w_out, p_gate, 0, gate_bias[:d_model], None, F32, 2048, 256)
    merged = _mm_gate(y, ssm_w_out, p_gate, d_model, gate_bias[d_model:], merged, BF16, 1024, 256)
    return _mm_residual(merged, w_o, x2d, 1.0, 2048, 256)


def kernel(x, ffn1_norm, ffn1_w_gate, ffn1_w_up, ffn1_w_down, mix_norm, w_in, gate_bias, sconv_w, sconv_w_out, ssm_conv_w, ssm_conv_b, ssm_dt_bias, ssm_A_log, ssm_D, ssm_norm, ssm_w_out, w_o, ffn2_norm, ffn2_w_gate, ffn2_w_up, ffn2_w_down, final_norm):
    bsz, length, d_model = x.shape
    x2d = x.reshape(bsz * length, d_model)
    for i in range(ffn1_norm.shape[0]):
        x2d = _swiglu_block(x2d, ffn1_norm[i], ffn1_w_gate[i], ffn1_w_up[i], ffn1_w_down[i])
        x2d = _mixer_block(x2d, bsz, length, mix_norm[i], w_in[i], gate_bias[i], sconv_w[i], sconv_w_out[i],
                           ssm_conv_w[i], ssm_conv_b[i], ssm_dt_bias[i], ssm_A_log[i], ssm_D[i], ssm_norm[i],
                           ssm_w_out[i], w_o[i])
        x2d = _swiglu_block(x2d, ffn2_norm[i], ffn2_w_gate[i], ffn2_w_up[i], ffn2_w_down[i])
    return _rmsnorm(x2d, final_norm, x.dtype).reshape(bsz, length, d_model)
```

```python
import functools
import math

import jax
import jax.numpy as jnp
from jax import lax
from jax.experimental import pallas as pl
from jax.experimental.pallas import tpu as pltpu

F32 = jnp.float32
BF16 = jnp.bfloat16

RMS_EPS = 1e-5
FFN_RESIDUAL_SCALE = 0.5
SSM_N_GROUPS = 8
SSM_D_STATE = 128
SSM_HEAD_DIM = 64
SSM_CHUNK = 128
SSM_GROUPS_PER_STEP = 4
LOG2_E = math.log2(math.e)
LANES = 128
SUBLANES = 8
VMEM_LIMIT_BYTES = 56 * 1024 * 1024
MM_ROW_CHUNK = 512
CONV_ROW_BLOCK = 32


def _params(*sem):
    return pltpu.CompilerParams(dimension_semantics=sem, vmem_limit_bytes=VMEM_LIMIT_BYTES)


def _tile(n, pref):
    if n <= pref:
        return n
    t = (pref // LANES) * LANES
    while t >= LANES:
        if n % t == 0:
            return t
        t -= LANES
    return n


def _silu(v):
    return v * jax.nn.sigmoid(v)


def _dot(a, b):
    return jnp.dot(a, b, preferred_element_type=F32)


def _rmsnorm_body(x_ref, w_ref, o_ref):
    x = x_ref[...]
    ms = jnp.mean(x * x, axis=-1, keepdims=True)
    o_ref[...] = (x * lax.rsqrt(ms + RMS_EPS) * w_ref[...]).astype(o_ref.dtype)


def _rmsnorm(x2d, w, out_dtype):
    m, d = x2d.shape
    tm = _tile(m, 256)
    return pl.pallas_call(
        _rmsnorm_body,
        grid=(m // tm,),
        in_specs=[pl.BlockSpec((tm, d), lambda i: (i, 0)), pl.BlockSpec((1, d), lambda i: (0, 0))],
        out_specs=pl.BlockSpec((tm, d), lambda i: (i, 0)),
        out_shape=jax.ShapeDtypeStruct((m, d), out_dtype),
        compiler_params=_params("parallel"),
        name="rmsnorm",
    )(x2d, w.reshape(1, d).astype(F32))


def _resident(block, index_map):
    return pl.BlockSpec(block, index_map, pipeline_mode=pl.Buffered(1))


def _w_spec(k, tn, col0):
    if col0 % tn == 0:
        joff = col0 // tn
        return pl.BlockSpec((k, tn), lambda i, j: (0, j + joff))
    assert col0 % LANES == 0 and tn % LANES == 0
    return pl.BlockSpec((pl.Element(k), pl.Element(tn)),
                        lambda i, j: (0, pl.multiple_of(col0 + j * tn, LANES)))


def _row_chunks(rows):
    step = min(rows, MM_ROW_CHUNK)
    assert rows % step == 0
    return [slice(r0, r0 + step) for r0 in range(0, rows, step)]


def _ffn_up_body(a_ref, wg_ref, wu_ref, o_ref):
    wg = wg_ref[...].astype(BF16)
    wu = wu_ref[...].astype(BF16)
    for rows in _row_chunks(a_ref.shape[0]):
        a = a_ref[rows, :]
        g = _dot(a, wg)
        u = _dot(a, wu)
        o_ref[rows, :] = (_silu(g) * u).astype(o_ref.dtype)


def _ffn_up(a, wg, wu):
    m, k = a.shape
    n = wg.shape[1]
    tm, tn = _tile(m, 2048), _tile(n, 256)
    return pl.pallas_call(
        _ffn_up_body,
        grid=(m // tm, n // tn),
        in_specs=[_a_spec(tm, k, True),
                  pl.BlockSpec((k, tn), lambda i, j: (0, j)),
                  pl.BlockSpec((k, tn), lambda i, j: (0, j))],
        out_specs=pl.BlockSpec((tm, tn), lambda i, j: (i, j)),
        out_shape=jax.ShapeDtypeStruct((m, n), BF16),
        compiler_params=_params("arbitrary", "arbitrary"),
        name="ffn_up",
    )(a, wg, wu)


def _mm_residual_body(a_ref, w_ref, r_ref, o_ref, *, scale):
    w = w_ref[...].astype(BF16)
    for rows in _row_chunks(a_ref.shape[0]):
        o_ref[rows, :] = r_ref[rows, :] + scale * _dot(a_ref[rows, :], w)


def _a_spec(tm, k, prefetch):
    if prefetch:
        return pl.BlockSpec((tm, k), lambda i, j: (i, 0))
    return _resident((tm, k), lambda i, j: (i, 0))


def _mm_residual(a, w, res, scale, tm_pref, tn_pref, prefetch_a=False):
    m, k = a.shape
    n = w.shape[1]
    tm, tn = _tile(m, tm_pref), _tile(n, tn_pref)
    return pl.pallas_call(
        functools.partial(_mm_residual_body, scale=scale),
        grid=(m // tm, n // tn),
        in_specs=[_a_spec(tm, k, prefetch_a),
                  pl.BlockSpec((k, tn), lambda i, j: (0, j)),
                  pl.BlockSpec((tm, tn), lambda i, j: (i, j))],
        out_specs=pl.BlockSpec((tm, tn), lambda i, j: (i, j)),
        out_shape=jax.ShapeDtypeStruct((m, n), F32),
        compiler_params=_params("arbitrary", "arbitrary"),
        name="mm_residual",
    )(a, w, res)


def _mm_plain_body(a_ref, w_ref, o_ref):
    w = w_ref[...].astype(BF16)
    for rows in _row_chunks(a_ref.shape[0]):
        o_ref[rows, :] = _dot(a_ref[rows, :], w).astype(o_ref.dtype)


def _mm_plain(a, w, col0, n, out_dtype):
    m, k = a.shape
    tm, tn = _tile(m, 2048), _tile(n, 512)
    return pl.pallas_call(
        _mm_plain_body,
        grid=(m // tm, n // tn),
        in_specs=[_resident((tm, k), lambda i, j: (i, 0)),
                  _w_spec(k, tn, col0)],
        out_specs=pl.BlockSpec((tm, tn), lambda i, j: (i, j)),
        out_shape=jax.ShapeDtypeStruct((m, n), out_dtype),
        compiler_params=_params("arbitrary", "arbitrary"),
        name="mm_plain",
    )(a, w)


def _mm_silu_side_body(a_ref, w_ref, ws_ref, o_ref, side_ref):
    w = w_ref[...].astype(BF16)
    for rows in _row_chunks(a_ref.shape[0]):
        o_ref[rows, :] = _silu(_dot(a_ref[rows, :], w)).astype(o_ref.dtype)

    @pl.when(pl.program_id(1) == 0)
    def _():
        ws = ws_ref[...].astype(BF16)
        for rows in _row_chunks(a_ref.shape[0]):
            side_ref[rows, :] = _dot(a_ref[rows, :], ws)


def _mm_silu_side(a, w, col0, n, side_col0, side_n):
    m, k = a.shape
    tm, tn = _tile(m, 2048), _tile(n, 512)
    assert side_col0 % side_n == 0 and side_n % LANES == 0
    side_blk = side_col0 // side_n
    return pl.pallas_call(
        _mm_silu_side_body,
        grid=(m // tm, n // tn),
        in_specs=[_resident((tm, k), lambda i, j: (i, 0)),
                  _w_spec(k, tn, col0),
                  _resident((k, side_n), lambda i, j: (0, side_blk))],
        out_specs=[pl.BlockSpec((tm, tn), lambda i, j: (i, j)),
                   pl.BlockSpec((tm, side_n), lambda i, j: (i, 0))],
        out_shape=[jax.ShapeDtypeStruct((m, n), BF16), jax.ShapeDtypeStruct((m, side_n), F32)],
        compiler_params=_params("arbitrary", "arbitrary"),
        name="mm_silu_side",
    )(a, w, w)


def _mm_conv_silu_body(a_ref, w_ref, cw_ref, cb_ref, o_ref, raw):
    w = w_ref[...].astype(BF16)
    taps = cw_ref[...]
    bias = cb_ref[...]
    kt = taps.shape[0]
    raw[0:SUBLANES, :] = jnp.zeros((SUBLANES, w.shape[1]), F32)
    for rows in _row_chunks(a_ref.shape[0]):
        raw[rows.start + SUBLANES:rows.stop + SUBLANES, :] = _dot(a_ref[rows, :], w)
        for r0 in range(rows.start, rows.stop, CONV_ROW_BLOCK):
            ext = raw[r0:r0 + CONV_ROW_BLOCK + SUBLANES, :]
            acc = ext[SUBLANES:] * taps[kt - 1:kt]
            for s in range(1, kt):
                acc = acc + pltpu.roll(ext, s, 0)[SUBLANES:] * taps[kt - 1 - s:kt - s]
            o_ref[r0:r0 + CONV_ROW_BLOCK, :] = _silu(acc + bias).astype(o_ref.dtype)


def _mm_conv_silu(a, w, col0, n, conv_w, conv_b, length):
    m, k = a.shape
    assert m % length == 0 and conv_w.shape[0] <= SUBLANES and min(length, MM_ROW_CHUNK) % CONV_ROW_BLOCK == 0
    tm, tn = length, _tile(n, 256)
    return pl.pallas_call(
        _mm_conv_silu_body,
        grid=(m // tm, n // tn),
        in_specs=[_a_spec(tm, k, True),
                  _w_spec(k, tn, col0),
                  pl.BlockSpec((conv_w.shape[0], tn), lambda i, j: (0, j)),
                  pl.BlockSpec((1, tn), lambda i, j: (0, j))],
        out_specs=pl.BlockSpec((tm, tn), lambda i, j: (i, j)),
        out_shape=jax.ShapeDtypeStruct((m, n), BF16),
        scratch_shapes=[pltpu.VMEM((tm + SUBLANES, tn), F32)],
        compiler_params=_params("arbitrary", "arbitrary"),
        name="mm_conv_silu",
    )(a, w, conv_w.astype(F32), conv_b.reshape(1, n).astype(F32))


def _mm_gate_body(a_ref, w_ref, g_ref, b_ref, o_ref):
    w = w_ref[...].astype(BF16)
    bias = b_ref[...]
    for rows in _row_chunks(a_ref.shape[0]):
        gate = jax.nn.sigmoid(g_ref[rows, :].astype(F32) + bias)
        o_ref[rows, :] = (gate * _dot(a_ref[rows, :], w)).astype(o_ref.dtype)


def _mm_gate_acc_body(a_ref, w_ref, g_ref, b_ref, p_ref, o_ref):
    w = w_ref[...].astype(BF16)
    bias = b_ref[...]
    for rows in _row_chunks(a_ref.shape[0]):
        gate = jax.nn.sigmoid(g_ref[rows, :].astype(F32) + bias)
        o_ref[rows, :] = (p_ref[rows, :] + gate * _dot(a_ref[rows, :], w)).astype(o_ref.dtype)


def _mm_gate(a, w, gpre, gcol0, bias, prev, out_dtype, tm_pref, tn_pref, prefetch_a=False):
    m, k = a.shape
    n = w.shape[1]
    tm, tn = _tile(m, tm_pref), _tile(n, tn_pref)
    joff = gcol0 // tn
    in_specs = [_a_spec(tm, k, prefetch_a),
                pl.BlockSpec((k, tn), lambda i, j: (0, j)),
                pl.BlockSpec((tm, tn), lambda i, j: (i, j + joff)),
                pl.BlockSpec((1, tn), lambda i, j: (0, j))]
    args = [a, w, gpre, bias.reshape(1, n).astype(F32)]
    body = _mm_gate_body
    if prev is not None:
        in_specs.append(pl.BlockSpec((tm, tn), lambda i, j: (i, j)))
        args.append(prev)
        body = _mm_gate_acc_body
    return pl.pallas_call(
        body,
        grid=(m // tm, n // tn),
        in_specs=in_specs,
        out_specs=pl.BlockSpec((tm, tn), lambda i, j: (i, j)),
        out_shape=jax.ShapeDtypeStruct((m, n), out_dtype),
        compiler_params=_params("arbitrary", "arbitrary"),
        name="mm_gate",
    )(*args)


def _sconv_body(b_ref, c_ref, x_ref, w_ref, o_ref):
    u = c_ref[0].astype(F32) * x_ref[0].astype(F32)
    row = lax.broadcasted_iota(jnp.int32, u.shape, 0)
    w = w_ref[...]
    k_taps = w.shape[0]
    y = None
    for k in range(k_taps):
        shift = k_taps - 1 - k
        us = u if shift == 0 else jnp.where(row >= shift, pltpu.roll(u, shift, 0), 0.0)
        term = us * w[k:k + 1]
        y = term if y is None else y + term
    o_ref[0] = (b_ref[0].astype(F32) * y).astype(o_ref.dtype)


def _sconv(p3, w, width):
    bsz, length, _ = p3.shape
    tc = _tile(width, 256)
    nb = width // tc
    return pl.pallas_call(
        _sconv_body,
        grid=(bsz, nb),
        in_specs=[pl.BlockSpec((1, length, tc), lambda b, j: (b, 0, j)),
                  pl.BlockSpec((1, length, tc), lambda b, j: (b, 0, j + nb)),
                  pl.BlockSpec((1, length, tc), lambda b, j: (b, 0, j + 2 * nb)),
                  pl.BlockSpec((w.shape[0], tc), lambda b, j: (0, j))],
        out_specs=pl.BlockSpec((1, length, tc), lambda b, j: (b, 0, j)),
        out_shape=jax.ShapeDtypeStruct((bsz, length, width), BF16),
        compiler_params=_params("parallel", "parallel"),
        name="sconv",
    )(p3, p3, p3, w.astype(F32))


def _split_bf16(v, parts):
    out = []
    rem = v
    for _ in range(parts):
        p = rem.astype(BF16)
        out.append(p)
        rem = rem - p.astype(F32)
    return jnp.concatenate(out, axis=1)


def _ssd_body(xs_ref, bm_ref, cm_ref, zs_ref, dt_ref, dtb_ref, alog_ref, dx_ref, nw_ref, e1_ref, e2_ref,
              y_ref, state, dtq_s, acs3_s, acst_s, *, heads_per_group, groups_per_step):
    c = pl.program_id(1)
    gi = pl.program_id(2)
    t = SSM_CHUNK
    n = SSM_D_STATE
    r = heads_per_group
    rp = r * SSM_HEAD_DIM
    gps = groups_per_step
    heads_per_tile = LANES // SSM_HEAD_DIM

    @pl.when(c == 0)
    def _():
        for s in range(gps):
            state[gi * gps + s] = jnp.zeros(state.shape[1:], F32)

    @pl.when(gi == 0)
    def _():
        v = dt_ref[0] + dtb_ref[...]
        dt = jnp.maximum(v, 0.0) + jnp.log1p(jnp.exp(-jnp.abs(v)))
        a = dt * (-jnp.exp(alog_ref[...]))
        row = lax.broadcasted_iota(jnp.int32, a.shape, 0)
        acs = a
        s = 1
        while s < t:
            acs = acs + jnp.where(row >= s, pltpu.roll(acs, s, 0), 0.0)
            s *= 2
        eacs = jnp.exp(acs)
        dec = jnp.exp(acs[t - 1:t, :] - acs)
        dtq_s[...] = _split_bf16(jnp.concatenate([dt, dec, eacs], axis=0), 2)
        acs2 = acs * LOG2_E
        acs3_s[...] = _split_bf16(acs2, 3)
        acst = acs2.T
        for gg in range(SSM_N_GROUPS):
            acst_s[gg] = acst[gg * r:(gg + 1) * r, :]

    tri = (lax.broadcasted_iota(jnp.int32, (t, t), 0) >= lax.broadcasted_iota(jnp.int32, (t, t), 1))
    lane = lax.broadcasted_iota(jnp.int32, (t, LANES), 1)
    dtq = dtq_s[...]
    acs3 = acs3_s[...]

    for s in range(gps):
        g = gi * gps + s
        ch = slice(s * rp, (s + 1) * rp)
        ex = _dot(dtq, e1_ref[:, ch])
        dtx, decx, eacsx = ex[0:t], ex[t:2 * t], ex[2 * t:3 * t]
        colb = _dot(acs3, e2_ref[:, s * r * t:(s + 1) * r * t])
        xs = xs_ref[0, :, ch].astype(F32)
        bmat = bm_ref[0, :, s * n:(s + 1) * n]
        cmat = cm_ref[0, :, s * n:(s + 1) * n]
        xdt = xs * dtx
        xdec_bf = (xdt * decx).astype(BF16)
        cb = lax.dot_general(cmat, bmat, (((1,), (1,)), ((), ())), preferred_element_type=F32)
        sprev = state[g]
        yoff = _dot(cmat, sprev.astype(BF16)) * eacsx
        snew = lax.dot_general(bmat, xdec_bf, (((0,), (0,)), ((), ())), preferred_element_type=F32)
        state[g] = eacsx[t - 1:t, :] * sprev + snew

        tiles = []
        ss = jnp.zeros((t, 1), F32)
        for q in range(rp // LANES):
            sl = slice(q * LANES, (q + 1) * LANES)
            xdt_q = xdt[:, sl]
            scores, xparts = [], []
            for e in range(heads_per_tile):
                k = q * heads_per_tile + e
                seg = colb[:, k * t:(k + 1) * t] - acst_s[g, pl.ds(k, 1), :]
                scores.append((cb * jnp.exp2(jnp.where(tri, seg, -jnp.inf))).astype(BF16))
                in_head = (lane >= e * SSM_HEAD_DIM) & (lane < (e + 1) * SSM_HEAD_DIM)
                xparts.append(jnp.where(in_head, xdt_q, 0.0).astype(BF16))
            ydiag = _dot(jnp.concatenate(scores, axis=1), jnp.concatenate(xparts, axis=0))
            acc = ydiag + yoff[:, sl] + dx_ref[:, s * rp + q * LANES:s * rp + (q + 1) * LANES] * xs[:, sl]
            gated = acc * zs_ref[0, :, s * rp + q * LANES:s * rp + (q + 1) * LANES].astype(F32)
            tiles.append(gated)
            ss = ss + jnp.sum(gated * gated, axis=-1, keepdims=True)
        inv = lax.rsqrt(ss / rp + RMS_EPS)
        for q, gated in enumerate(tiles):
            cs = slice(s * rp + q * LANES, s * rp + (q + 1) * LANES)
            y_ref[0, :, cs] = (gated * inv * nw_ref[:, cs]).astype(y_ref.dtype)


def _ssd(xbc, zs, dt_raw, dt_bias, a_log, d_skip, norm_w):
    bsz, length, d_inner = zs.shape
    n_heads = dt_raw.shape[-1]
    g_, n_, t = SSM_N_GROUPS, SSM_D_STATE, SSM_CHUNK
    r = n_heads // g_
    rp = r * SSM_HEAD_DIM
    gps = SSM_GROUPS_PER_STEP
    assert d_inner == n_heads * SSM_HEAD_DIM and rp % LANES == 0 and n_ == LANES and length % t == 0
    assert g_ % gps == 0
    wx, wn = gps * rp, gps * n_
    xb = d_inner // wn
    head_of_ch = jnp.arange(d_inner, dtype=jnp.int32) // SSM_HEAD_DIM
    e1 = (jnp.arange(n_heads, dtype=jnp.int32)[:, None] == head_of_ch[None, :]).astype(BF16)
    head_of_col = jnp.arange(n_heads * t, dtype=jnp.int32) // t
    e2 = (jnp.arange(n_heads, dtype=jnp.int32)[:, None] == head_of_col[None, :]).astype(BF16)
    e1 = jnp.tile(e1, (2, 1))
    e2 = jnp.tile(e2, (3, 1))
    d_x = jnp.repeat(d_skip.astype(F32), SSM_HEAD_DIM).reshape(1, d_inner)

    in_specs = [
        pl.BlockSpec((1, t, wx), lambda b, c, g: (b, c, g)),
        pl.BlockSpec((1, t, wn), lambda b, c, g: (b, c, xb + g)),
        pl.BlockSpec((1, t, wn), lambda b, c, g: (b, c, xb + g_ // gps + g)),
        pl.BlockSpec((1, t, wx), lambda b, c, g: (b, c, g)),
        pl.BlockSpec((1, t, n_heads), lambda b, c, g: (b, c, 0)),
        pl.BlockSpec((1, n_heads), lambda b, c, g: (0, 0)),
        pl.BlockSpec((1, n_heads), lambda b, c, g: (0, 0)),
        pl.BlockSpec((1, wx), lambda b, c, g: (0, g)),
        pl.BlockSpec((1, wx), lambda b, c, g: (0, g)),
        pl.BlockSpec((2 * n_heads, wx), lambda b, c, g: (0, g)),
        pl.BlockSpec((3 * n_heads, gps * r * t), lambda b, c, g: (0, g)),
    ]
    scratch = [
        pltpu.VMEM((g_, n_, rp), F32),
        pltpu.VMEM((3 * t, 2 * n_heads), BF16),
        pltpu.VMEM((t, 3 * n_heads), BF16),
        pltpu.VMEM((g_, r, t), F32),
    ]
    return pl.pallas_call(
        functools.partial(_ssd_body, heads_per_group=r, groups_per_step=gps),
        grid=(bsz, length // t, g_ // gps),
        in_specs=in_specs,
        out_specs=pl.BlockSpec((1, t, wx), lambda b, c, g: (b, c, g)),
        out_shape=jax.ShapeDtypeStruct((bsz, length, d_inner), BF16),
        scratch_shapes=scratch,
        compiler_params=_params("arbitrary", "arbitrary", "arbitrary"),
        name="ssd",
    )(xbc, xbc, xbc, zs, dt_raw,
      dt_bias.reshape(1, -1).astype(F32), a_log.reshape(1, -1).astype(F32), d_x,
      norm_w.reshape(1, -1).astype(F32), e1, e2)


def _swiglu_block(x2d, norm_w, w_gate, w_up, w_down):
    h = _rmsnorm(x2d, norm_w, BF16)
    act = _ffn_up(h, w_gate, w_up)
    return _mm_residual(act, w_down, x2d, FFN_RESIDUAL_SCALE, 1024, 256)


def _mixer_block(x2d, bsz, length, norm_w, w_in, gate_bias, sconv_w, sconv_w_out, ssm_conv_w, ssm_conv_b,
                 ssm_dt_bias, ssm_a_log, ssm_d, ssm_norm, ssm_w_out, w_o):
    d_model = x2d.shape[1]
    conv_width = sconv_w.shape[1]
    d_inner = ssm_norm.shape[0]
    conv_dim = ssm_conv_w.shape[1]
    n_heads = ssm_a_log.shape[0]
    o_z = 3 * conv_width
    o_xbc = o_z + d_inner
    o_dt = o_xbc + conv_dim
    o_g = o_dt + n_heads

    h = _rmsnorm(x2d, norm_w, BF16)
    p_conv = _mm_plain(h, w_in, 0, o_z, BF16)
    zs, p_dt = _mm_silu_side(h, w_in, o_z, d_inner, o_dt, n_heads)
    xbc = _mm_conv_silu(h, w_in, o_xbc, conv_dim, ssm_conv_w, ssm_conv_b, length)
    p_gate = _mm_plain(h, w_in, o_g, 2 * d_model, BF16)

    v = _sconv(p_conv.reshape(bsz, length, o_z), sconv_w, conv_width).reshape(bsz * length, conv_width)
    y = _ssd(xbc.reshape(bsz, length, conv_dim), zs.reshape(bsz, length, d_inner),
             p_dt.reshape(bsz, length, n_heads), ssm_dt_bias, ssm_a_log, ssm_d,
             ssm_norm).reshape(bsz * length, d_inner)

    merged = _mm_gate(v, sconv_w_out, p_gate, 0, gate_bias[:d_model], None, F32, 2048, 256, prefetch_a=True)
    merged = _mm_gate(y, ssm_w_out, p_gate, d_model, gate_bias[d_model:], merged, BF16, 1024, 256)
    return _mm_residual(merged, w_o, x2d, 1.0, 2048, 256, prefetch_a=True)


def kernel(x, ffn1_norm, ffn1_w_gate, ffn1_w_up, ffn1_w_down, mix_norm, w_in, gate_bias, sconv_w, sconv_w_out, ssm_conv_w, ssm_conv_b, ssm_dt_bias, ssm_A_log, ssm_D, ssm_norm, ssm_w_out, w_o, ffn2_norm, ffn2_w_gate, ffn2_w_up, ffn2_w_down, final_norm):
    bsz, length, d_model = x.shape
    x2d = x.reshape(bsz * length, d_model)
    for i in range(ffn1_norm.shape[0]):
        x2d = _swiglu_block(x2d, ffn1_norm[i], ffn1_w_gate[i], ffn1_w_up[i], ffn1_w_down[i])
        x2d = _mixer_block(x2d, bsz, length, mix_norm[i], w_in[i], gate_bias[i], sconv_w[i], sconv_w_out[i],
                           ssm_conv_w[i], ssm_conv_b[i], ssm_dt_bias[i], ssm_A_log[i], ssm_D[i], ssm_norm[i],
                           ssm_w_out[i], w_o[i])
        x2d = _swiglu_block(x2d, ffn2_norm[i], ffn2_w_gate[i], ffn2_w_up[i], ffn2_w_down[i])
    return _rmsnorm(x2d, final_norm, x.dtype).reshape(bsz, length, d_model)
```

```python
import functools
import math

import jax
import jax.numpy as jnp
from jax import lax
from jax.experimental import pallas as pl
from jax.experimental.pallas import tpu as pltpu

F32 = jnp.float32
BF16 = jnp.bfloat16

RMS_EPS = 1e-5
FFN_RESIDUAL_SCALE = 0.5
SSM_N_GROUPS = 8
SSM_D_STATE = 128
SSM_HEAD_DIM = 64
SSM_CHUNK = 128
SSM_GROUPS_PER_STEP = 8
LOG2_E = math.log2(math.e)
LANES = 128
SUBLANES = 8
VMEM_LIMIT_BYTES = 56 * 1024 * 1024
MM_ROW_CHUNK = 512
CONV_ROW_BLOCK = 32


def _params(*sem):
    return pltpu.CompilerParams(dimension_semantics=sem, vmem_limit_bytes=VMEM_LIMIT_BYTES)


def _tile(n, pref):
    if n <= pref:
        return n
    t = (pref // LANES) * LANES
    while t >= LANES:
        if n % t == 0:
            return t
        t -= LANES
    return n


def _silu(v):
    return v * jax.nn.sigmoid(v)


def _dot(a, b):
    return jnp.dot(a, b, preferred_element_type=F32)


def _rmsnorm_body(x_ref, w_ref, o_ref):
    x = x_ref[...]
    ms = jnp.mean(x * x, axis=-1, keepdims=True)
    o_ref[...] = (x * lax.rsqrt(ms + RMS_EPS) * w_ref[...]).astype(o_ref.dtype)


def _rmsnorm(x2d, w, out_dtype):
    m, d = x2d.shape
    tm = _tile(m, 256)
    return pl.pallas_call(
        _rmsnorm_body,
        grid=(m // tm,),
        in_specs=[pl.BlockSpec((tm, d), lambda i: (i, 0)), pl.BlockSpec((1, d), lambda i: (0, 0))],
        out_specs=pl.BlockSpec((tm, d), lambda i: (i, 0)),
        out_shape=jax.ShapeDtypeStruct((m, d), out_dtype),
        compiler_params=_params("parallel"),
        name="rmsnorm",
    )(x2d, w.reshape(1, d).astype(F32))


def _resident(block, index_map):
    return pl.BlockSpec(block, index_map, pipeline_mode=pl.Buffered(1))


def _w_spec(k, tn, col0):
    if col0 % tn == 0:
        joff = col0 // tn
        return pl.BlockSpec((k, tn), lambda i, j: (0, j + joff))
    assert col0 % LANES == 0 and tn % LANES == 0
    return pl.BlockSpec((pl.Element(k), pl.Element(tn)),
                        lambda i, j: (0, pl.multiple_of(col0 + j * tn, LANES)))


def _row_chunks(rows):
    step = min(rows, MM_ROW_CHUNK)
    assert rows % step == 0
    return [slice(r0, r0 + step) for r0 in range(0, rows, step)]


def _ffn_up_body(a_ref, wg_ref, wu_ref, o_ref):
    wg = wg_ref[...].astype(BF16)
    wu = wu_ref[...].astype(BF16)
    for rows in _row_chunks(a_ref.shape[0]):
        a = a_ref[rows, :]
        g = _dot(a, wg)
        u = _dot(a, wu)
        o_ref[rows, :] = (_silu(g) * u).astype(o_ref.dtype)


def _ffn_up(a, wg, wu):
    m, k = a.shape
    n = wg.shape[1]
    tm, tn = _tile(m, 2048), _tile(n, 256)
    return pl.pallas_call(
        _ffn_up_body,
        grid=(m // tm, n // tn),
        in_specs=[_a_spec(tm, k, True),
                  pl.BlockSpec((k, tn), lambda i, j: (0, j)),
                  pl.BlockSpec((k, tn), lambda i, j: (0, j))],
        out_specs=pl.BlockSpec((tm, tn), lambda i, j: (i, j)),
        out_shape=jax.ShapeDtypeStruct((m, n), BF16),
        compiler_params=_params("arbitrary", "arbitrary"),
        name="ffn_up",
    )(a, wg, wu)


def _mm_residual_body(a_ref, w_ref, r_ref, o_ref, *, scale):
    w = w_ref[...].astype(BF16)
    for rows in _row_chunks(a_ref.shape[0]):
        o_ref[rows, :] = r_ref[rows, :] + scale * _dot(a_ref[rows, :], w)


def _a_spec(tm, k, prefetch):
    if prefetch:
        return pl.BlockSpec((tm, k), lambda i, j: (i, 0))
    return _resident((tm, k), lambda i, j: (i, 0))


def _mm_residual(a, w, res, scale, tm_pref, tn_pref, prefetch_a=False):
    m, k = a.shape
    n = w.shape[1]
    tm, tn = _tile(m, tm_pref), _tile(n, tn_pref)
    return pl.pallas_call(
        functools.partial(_mm_residual_body, scale=scale),
        grid=(m // tm, n // tn),
        in_specs=[_a_spec(tm, k, prefetch_a),
                  pl.BlockSpec((k, tn), lambda i, j: (0, j)),
                  pl.BlockSpec((tm, tn), lambda i, j: (i, j))],
        out_specs=pl.BlockSpec((tm, tn), lambda i, j: (i, j)),
        out_shape=jax.ShapeDtypeStruct((m, n), F32),
        compiler_params=_params("arbitrary", "arbitrary"),
        name="mm_residual",
    )(a, w, res)


def _mm_plain_body(a_ref, w_ref, o_ref):
    w = w_ref[...].astype(BF16)
    for rows in _row_chunks(a_ref.shape[0]):
        o_ref[rows, :] = _dot(a_ref[rows, :], w).astype(o_ref.dtype)


def _mm_plain(a, w, col0, n, out_dtype):
    m, k = a.shape
    tm, tn = _tile(m, 2048), _tile(n, 512)
    return pl.pallas_call(
        _mm_plain_body,
        grid=(m // tm, n // tn),
        in_specs=[_resident((tm, k), lambda i, j: (i, 0)),
                  _w_spec(k, tn, col0)],
        out_specs=pl.BlockSpec((tm, tn), lambda i, j: (i, j)),
        out_shape=jax.ShapeDtypeStruct((m, n), out_dtype),
        compiler_params=_params("arbitrary", "arbitrary"),
        name="mm_plain",
    )(a, w)


def _mm_silu_side_body(a_ref, w_ref, ws_ref, o_ref, side_ref):
    w = w_ref[...].astype(BF16)
    for rows in _row_chunks(a_ref.shape[0]):
        o_ref[rows, :] = _silu(_dot(a_ref[rows, :], w)).astype(o_ref.dtype)

    @pl.when(pl.program_id(1) == 0)
    def _():
        ws = ws_ref[...].astype(BF16)
        for rows in _row_chunks(a_ref.shape[0]):
            side_ref[rows, :] = _dot(a_ref[rows, :], ws)


def _mm_silu_side(a, w, col0, n, side_col0, side_n):
    m, k = a.shape
    tm, tn = _tile(m, 2048), _tile(n, 512)
    assert side_col0 % side_n == 0 and side_n % LANES == 0
    side_blk = side_col0 // side_n
    return pl.pallas_call(
        _mm_silu_side_body,
        grid=(m // tm, n // tn),
        in_specs=[_resident((tm, k), lambda i, j: (i, 0)),
                  _w_spec(k, tn, col0),
                  _resident((k, side_n), lambda i, j: (0, side_blk))],
        out_specs=[pl.BlockSpec((tm, tn), lambda i, j: (i, j)),
                   pl.BlockSpec((tm, side_n), lambda i, j: (i, 0))],
        out_shape=[jax.ShapeDtypeStruct((m, n), BF16), jax.ShapeDtypeStruct((m, side_n), F32)],
        compiler_params=_params("arbitrary", "arbitrary"),
        name="mm_silu_side",
    )(a, w, w)


def _mm_conv_silu_body(a_ref, w_ref, cw_ref, cb_ref, o_ref, raw):
    w = w_ref[...].astype(BF16)
    taps = cw_ref[...]
    bias = cb_ref[...]
    kt = taps.shape[0]
    raw[0:SUBLANES, :] = jnp.zeros((SUBLANES, w.shape[1]), F32)
    for rows in _row_chunks(a_ref.shape[0]):
        raw[rows.start + SUBLANES:rows.stop + SUBLANES, :] = _dot(a_ref[rows, :], w)
        for r0 in range(rows.start, rows.stop, CONV_ROW_BLOCK):
            ext = raw[r0:r0 + CONV_ROW_BLOCK + SUBLANES, :]
            acc = ext[SUBLANES:] * taps[kt - 1:kt]
            for s in range(1, kt):
                acc = acc + pltpu.roll(ext, s, 0)[SUBLANES:] * taps[kt - 1 - s:kt - s]
            o_ref[r0:r0 + CONV_ROW_BLOCK, :] = _silu(acc + bias).astype(o_ref.dtype)


def _mm_conv_silu(a, w, col0, n, conv_w, conv_b, length):
    m, k = a.shape
    assert m % length == 0 and conv_w.shape[0] <= SUBLANES and min(length, MM_ROW_CHUNK) % CONV_ROW_BLOCK == 0
    tm, tn = length, _tile(n, 256)
    return pl.pallas_call(
        _mm_conv_silu_body,
        grid=(m // tm, n // tn),
        in_specs=[_a_spec(tm, k, True),
                  _w_spec(k, tn, col0),
                  pl.BlockSpec((conv_w.shape[0], tn), lambda i, j: (0, j)),
                  pl.BlockSpec((1, tn), lambda i, j: (0, j))],
        out_specs=pl.BlockSpec((tm, tn), lambda i, j: (i, j)),
        out_shape=jax.ShapeDtypeStruct((m, n), BF16),
        scratch_shapes=[pltpu.VMEM((tm + SUBLANES, tn), F32)],
        compiler_params=_params("arbitrary", "arbitrary"),
        name="mm_conv_silu",
    )(a, w, conv_w.astype(F32), conv_b.reshape(1, n).astype(F32))


def _mm_gate_body(a_ref, w_ref, g_ref, b_ref, o_ref):
    w = w_ref[...].astype(BF16)
    bias = b_ref[...]
    for rows in _row_chunks(a_ref.shape[0]):
        gate = jax.nn.sigmoid(g_ref[rows, :].astype(F32) + bias)
        o_ref[rows, :] = (gate * _dot(a_ref[rows, :], w)).astype(o_ref.dtype)


def _mm_gate_acc_body(a_ref, w_ref, g_ref, b_ref, p_ref, o_ref):
    w = w_ref[...].astype(BF16)
    bias = b_ref[...]
    for rows in _row_chunks(a_ref.shape[0]):
        gate = jax.nn.sigmoid(g_ref[rows, :].astype(F32) + bias)
        o_ref[rows, :] = (p_ref[rows, :] + gate * _dot(a_ref[rows, :], w)).astype(o_ref.dtype)


def _mm_gate(a, w, gpre, gcol0, bias, prev, out_dtype, tm_pref, tn_pref, prefetch_a=False):
    m, k = a.shape
    n = w.shape[1]
    tm, tn = _tile(m, tm_pref), _tile(n, tn_pref)
    joff = gcol0 // tn
    in_specs = [_a_spec(tm, k, prefetch_a),
                pl.BlockSpec((k, tn), lambda i, j: (0, j)),
                pl.BlockSpec((tm, tn), lambda i, j: (i, j + joff)),
                pl.BlockSpec((1, tn), lambda i, j: (0, j))]
    args = [a, w, gpre, bias.reshape(1, n).astype(F32)]
    body = _mm_gate_body
    if prev is not None:
        in_specs.append(pl.BlockSpec((tm, tn), lambda i, j: (i, j)))
        args.append(prev)
        body = _mm_gate_acc_body
    return pl.pallas_call(
        body,
        grid=(m // tm, n // tn),
        in_specs=in_specs,
        out_specs=pl.BlockSpec((tm, tn), lambda i, j: (i, j)),
        out_shape=jax.ShapeDtypeStruct((m, n), out_dtype),
        compiler_params=_params("arbitrary", "arbitrary"),
        name="mm_gate",
    )(*args)


def _sconv_body(b_ref, c_ref, x_ref, w_ref, o_ref):
    u = c_ref[0].astype(F32) * x_ref[0].astype(F32)
    row = lax.broadcasted_iota(jnp.int32, u.shape, 0)
    w = w_ref[...]
    k_taps = w.shape[0]
    y = None
    for k in range(k_taps):
        shift = k_taps - 1 - k
        us = u if shift == 0 else jnp.where(row >= shift, pltpu.roll(u, shift, 0), 0.0)
        term = us * w[k:k + 1]
        y = term if y is None else y + term
    o_ref[0] = (b_ref[0].astype(F32) * y).astype(o_ref.dtype)


def _sconv(p3, w, width):
    bsz, length, _ = p3.shape
    tc = _tile(width, 256)
    nb = width // tc
    return pl.pallas_call(
        _sconv_body,
        grid=(bsz, nb),
        in_specs=[pl.BlockSpec((1, length, tc), lambda b, j: (b, 0, j)),
                  pl.BlockSpec((1, length, tc), lambda b, j: (b, 0, j + nb)),
                  pl.BlockSpec((1, length, tc), lambda b, j: (b, 0, j + 2 * nb)),
                  pl.BlockSpec((w.shape[0], tc), lambda b, j: (0, j))],
        out_specs=pl.BlockSpec((1, length, tc), lambda b, j: (b, 0, j)),
        out_shape=jax.ShapeDtypeStruct((bsz, length, width), BF16),
        compiler_params=_params("parallel", "parallel"),
        name="sconv",
    )(p3, p3, p3, w.astype(F32))


def _split_bf16(v, parts):
    out = []
    rem = v
    for _ in range(parts):
        p = rem.astype(BF16)
        out.append(p)
        rem = rem - p.astype(F32)
    return jnp.concatenate(out, axis=1)


def _ssd_body(xs_ref, bm_ref, cm_ref, zs_ref, dt_ref, dtb_ref, alog_ref, dx_ref, nw_ref, e1_ref, e2_ref,
              y_ref, state, wq_s, acs3_s, rowt_s, *, heads_per_group, groups_per_step):
    c = pl.program_id(1)
    gi = pl.program_id(2)
    t = SSM_CHUNK
    n = SSM_D_STATE
    r = heads_per_group
    rp = r * SSM_HEAD_DIM
    gps = groups_per_step
    heads_per_tile = LANES // SSM_HEAD_DIM

    @pl.when(c == 0)
    def _():
        for s in range(gps):
            state[gi * gps + s] = jnp.zeros(state.shape[1:], F32)

    @pl.when(gi == 0)
    def _():
        v = dt_ref[0] + dtb_ref[...]
        dt = jnp.maximum(v, 0.0) + jnp.log1p(jnp.exp(-jnp.abs(v)))
        a = dt * (-jnp.exp(alog_ref[...]))
        row = lax.broadcasted_iota(jnp.int32, a.shape, 0)
        acs = a
        s = 1
        while s < t:
            acs = acs + jnp.where(row >= s, pltpu.roll(acs, s, 0), 0.0)
            s *= 2
        eacs = jnp.exp(acs)
        wdec = dt * jnp.exp(acs[t - 1:t, :] - acs)
        wq_s[...] = _split_bf16(jnp.concatenate([wdec, eacs], axis=0), 2)
        acs2 = acs * LOG2_E
        acs3_s[...] = _split_bf16(acs2, 3)
        rowt = (acs2 - jnp.log2(dt)).T
        for gg in range(SSM_N_GROUPS):
            rowt_s[gg] = rowt[gg * r:(gg + 1) * r, :]

    tri = (lax.broadcasted_iota(jnp.int32, (t, t), 0) >= lax.broadcasted_iota(jnp.int32, (t, t), 1))
    lane = lax.broadcasted_iota(jnp.int32, (t, LANES), 1)
    wq = wq_s[...]
    acs3 = acs3_s[...]

    for s in range(gps):
        g = gi * gps + s
        ch = slice(s * rp, (s + 1) * rp)
        ex = _dot(wq, e1_ref[:, ch])
        wx, eacsx = ex[0:t], ex[t:2 * t]
        colb = _dot(acs3, e2_ref[:, s * r * t:(s + 1) * r * t])
        xs = xs_ref[0, :, ch].astype(F32)
        bmat = bm_ref[0, :, s * n:(s + 1) * n]
        cmat = cm_ref[0, :, s * n:(s + 1) * n]
        xdec_bf = (xs * wx).astype(BF16)
        cb = lax.dot_general(cmat, bmat, (((1,), (1,)), ((), ())), preferred_element_type=F32)
        sprev = state[g]
        yoff = _dot(cmat, sprev.astype(BF16)) * eacsx
        snew = lax.dot_general(bmat, xdec_bf, (((0,), (0,)), ((), ())), preferred_element_type=F32)
        state[g] = eacsx[t - 1:t, :] * sprev + snew

        tiles = []
        ss = jnp.zeros((t, 1), F32)
        for q in range(rp // LANES):
            sl = slice(q * LANES, (q + 1) * LANES)
            cs = slice(s * rp + q * LANES, s * rp + (q + 1) * LANES)
            xs_q = xs[:, sl]
            scores, xparts = [], []
            for e in range(heads_per_tile):
                k = q * heads_per_tile + e
                seg = colb[:, k * t:(k + 1) * t] - rowt_s[g, pl.ds(k, 1), :]
                scores.append((cb * jnp.exp2(jnp.where(tri, seg, -jnp.inf))).astype(BF16))
                in_head = (lane >= e * SSM_HEAD_DIM) & (lane < (e + 1) * SSM_HEAD_DIM)
                xparts.append(jnp.where(in_head, xs_q, 0.0).astype(BF16))
            ydiag = _dot(jnp.concatenate(scores, axis=1), jnp.concatenate(xparts, axis=0))
            acc = ydiag + yoff[:, sl] + dx_ref[:, cs] * xs_q
            gated = acc * zs_ref[0, :, cs].astype(F32)
            tiles.append(gated)
            ss = ss + jnp.sum(gated * gated, axis=-1, keepdims=True)
        inv = lax.rsqrt(ss / rp + RMS_EPS)
        for q, gated in enumerate(tiles):
            cs = slice(s * rp + q * LANES, s * rp + (q + 1) * LANES)
            y_ref[0, :, cs] = (gated * inv * nw_ref[:, cs]).astype(y_ref.dtype)


def _ssd(xbc, zs, dt_raw, dt_bias, a_log, d_skip, norm_w):
    bsz, length, d_inner = zs.shape
    n_heads = dt_raw.shape[-1]
    g_, n_, t = SSM_N_GROUPS, SSM_D_STATE, SSM_CHUNK
    r = n_heads // g_
    rp = r * SSM_HEAD_DIM
    gps = SSM_GROUPS_PER_STEP
    assert d_inner == n_heads * SSM_HEAD_DIM and rp % LANES == 0 and n_ == LANES and length % t == 0
    assert g_ % gps == 0
    wx, wn = gps * rp, gps * n_
    xb = d_inner // wn
    head_of_ch = jnp.arange(d_inner, dtype=jnp.int32) // SSM_HEAD_DIM
    e1 = (jnp.arange(n_heads, dtype=jnp.int32)[:, None] == head_of_ch[None, :]).astype(BF16)
    head_of_col = jnp.arange(n_heads * t, dtype=jnp.int32) // t
    e2 = (jnp.arange(n_heads, dtype=jnp.int32)[:, None] == head_of_col[None, :]).astype(BF16)
    e1 = jnp.tile(e1, (2, 1))
    e2 = jnp.tile(e2, (3, 1))
    d_x = jnp.repeat(d_skip.astype(F32), SSM_HEAD_DIM).reshape(1, d_inner)
    const_mode = pl.Buffered(1) if gps == g_ else None

    in_specs = [
        pl.BlockSpec((1, t, wx), lambda b, c, g: (b, c, g)),
        pl.BlockSpec((1, t, wn), lambda b, c, g: (b, c, xb + g)),
        pl.BlockSpec((1, t, wn), lambda b, c, g: (b, c, xb + g_ // gps + g)),
        pl.BlockSpec((1, t, wx), lambda b, c, g: (b, c, g)),
        pl.BlockSpec((1, t, n_heads), lambda b, c, g: (b, c, 0)),
        pl.BlockSpec((1, n_heads), lambda b, c, g: (0, 0)),
        pl.BlockSpec((1, n_heads), lambda b, c, g: (0, 0)),
        pl.BlockSpec((1, wx), lambda b, c, g: (0, g)),
        pl.BlockSpec((1, wx), lambda b, c, g: (0, g)),
        pl.BlockSpec((2 * n_heads, wx), lambda b, c, g: (0, g), pipeline_mode=const_mode),
        pl.BlockSpec((3 * n_heads, gps * r * t), lambda b, c, g: (0, g), pipeline_mode=const_mode),
    ]
    scratch = [
        pltpu.VMEM((g_, n_, rp), F32),
        pltpu.VMEM((2 * t, 2 * n_heads), BF16),
        pltpu.VMEM((t, 3 * n_heads), BF16),
        pltpu.VMEM((g_, r, t), F32),
    ]
    return pl.pallas_call(
        functools.partial(_ssd_body, heads_per_group=r, groups_per_step=gps),
        grid=(bsz, length // t, g_ // gps),
        in_specs=in_specs,
        out_specs=pl.BlockSpec((1, t, wx), lambda b, c, g: (b, c, g)),
        out_shape=jax.ShapeDtypeStruct((bsz, length, d_inner), BF16),
        scratch_shapes=scratch,
        compiler_params=_params("arbitrary", "arbitrary", "arbitrary"),
        name="ssd",
    )(xbc, xbc, xbc, zs, dt_raw,
      dt_bias.reshape(1, -1).astype(F32), a_log.reshape(1, -1).astype(F32), d_x,
      norm_w.reshape(1, -1).astype(F32), e1, e2)


def _swiglu_block(x2d, norm_w, w_gate, w_up, w_down):
    h = _rmsnorm(x2d, norm_w, BF16)
    act = _ffn_up(h, w_gate, w_up)
    return _mm_residual(act, w_down, x2d, FFN_RESIDUAL_SCALE, 1024, 256)


def _mixer_block(x2d, bsz, length, norm_w, w_in, gate_bias, sconv_w, sconv_w_out, ssm_conv_w, ssm_conv_b,
                 ssm_dt_bias, ssm_a_log, ssm_d, ssm_norm, ssm_w_out, w_o):
    d_model = x2d.shape[1]
    conv_width = sconv_w.shape[1]
    d_inner = ssm_norm.shape[0]
    conv_dim = ssm_conv_w.shape[1]
    n_heads = ssm_a_log.shape[0]
    o_z = 3 * conv_width
    o_xbc = o_z + d_inner
    o_dt = o_xbc + conv_dim
    o_g = o_dt + n_heads

    h = _rmsnorm(x2d, norm_w, BF16)
    p_conv = _mm_plain(h, w_in, 0, o_z, BF16)
    zs, p_dt = _mm_silu_side(h, w_in, o_z, d_inner, o_dt, n_heads)
    xbc = _mm_conv_silu(h, w_in, o_xbc, conv_dim, ssm_conv_w, ssm_conv_b, length)
    p_gate = _mm_plain(h, w_in, o_g, 2 * d_model, BF16)

    v = _sconv(p_conv.reshape(bsz, length, o_z), sconv_w, conv_width).reshape(bsz * length, conv_width)
    y = _ssd(xbc.reshape(bsz, length, conv_dim), zs.reshape(bsz, length, d_inner),
             p_dt.reshape(bsz, length, n_heads), ssm_dt_bias, ssm_a_log, ssm_d,
             ssm_norm).reshape(bsz * length, d_inner)

    merged = _mm_gate(v, sconv_w_out, p_gate, 0, gate_bias[:d_model], None, F32, 2048, 256, prefetch_a=True)
    merged = _mm_gate(y, ssm_w_out, p_gate, d_model, gate_bias[d_model:], merged, BF16, 1024, 256, prefetch_a=True)
    return _mm_residual(merged, w_o, x2d, 1.0, 2048, 256, prefetch_a=True)


def kernel(x, ffn1_norm, ffn1_w_gate, ffn1_w_up, ffn1_w_down, mix_norm, w_in, gate_bias, sconv_w, sconv_w_out, ssm_conv_w, ssm_conv_b, ssm_dt_bias, ssm_A_log, ssm_D, ssm_norm, ssm_w_out, w_o, ffn2_norm, ffn2_w_gate, ffn2_w_up, ffn2_w_down, final_norm):
    bsz, length, d_model = x.shape
    x2d = x.reshape(bsz * length, d_model)
    for i in range(ffn1_norm.shape[0]):
        x2d = _swiglu_block(x2d, ffn1_norm[i], ffn1_w_gate[i], ffn1_w_up[i], ffn1_w_down[i])
        x2d = _mixer_block(x2d, bsz, length, mix_norm[i], w_in[i], gate_bias[i], sconv_w[i], sconv_w_out[i],
                           ssm_conv_w[i], ssm_conv_b[i], ssm_dt_bias[i], ssm_A_log[i], ssm_D[i], ssm_norm[i],
                           ssm_w_out[i], w_o[i])
        x2d = _swiglu_block(x2d, ffn2_norm[i], ffn2_w_gate[i], ffn2_w_up[i], ffn2_w_down[i])
    return _rmsnorm(x2d, final_norm, x.dtype).reshape(bsz, length, d_model)
```

```python
import functools
import math

import jax
import jax.numpy as jnp
from jax import lax
from jax.experimental import pallas as pl
from jax.experimental.pallas import tpu as pltpu

F32 = jnp.float32
BF16 = jnp.bfloat16

RMS_EPS = 1e-5
FFN_RESIDUAL_SCALE = 0.5
SSM_N_GROUPS = 8
SSM_D_STATE = 128
SSM_HEAD_DIM = 64
SSM_CHUNK = 128
SSM_GROUPS_PER_STEP = 8
LOG2_E = math.log2(math.e)
LANES = 128
SUBLANES = 8
VMEM_LIMIT_BYTES = 56 * 1024 * 1024
MM_ROW_CHUNK = 512
CONV_ROW_BLOCK = 32


def _params(*sem):
    return pltpu.CompilerParams(dimension_semantics=sem, vmem_limit_bytes=VMEM_LIMIT_BYTES)


def _tile(n, pref):
    if n <= pref:
        return n
    t = (pref // LANES) * LANES
    while t >= LANES:
        if n % t == 0:
            return t
        t -= LANES
    return n


def _silu(v):
    return v * jax.nn.sigmoid(v)


def _dot(a, b):
    return jnp.dot(a, b, preferred_element_type=F32)


def _rmsnorm_body(x_ref, w_ref, o_ref):
    x = x_ref[...]
    ms = jnp.mean(x * x, axis=-1, keepdims=True)
    o_ref[...] = (x * lax.rsqrt(ms + RMS_EPS) * w_ref[...]).astype(o_ref.dtype)


def _rmsnorm(x2d, w, out_dtype):
    m, d = x2d.shape
    tm = _tile(m, 256)
    return pl.pallas_call(
        _rmsnorm_body,
        grid=(m // tm,),
        in_specs=[pl.BlockSpec((tm, d), lambda i: (i, 0)), pl.BlockSpec((1, d), lambda i: (0, 0))],
        out_specs=pl.BlockSpec((tm, d), lambda i: (i, 0)),
        out_shape=jax.ShapeDtypeStruct((m, d), out_dtype),
        compiler_params=_params("parallel"),
        name="rmsnorm",
    )(x2d, w.reshape(1, d).astype(F32))


def _resident(block, index_map):
    return pl.BlockSpec(block, index_map, pipeline_mode=pl.Buffered(1))


def _w_spec(k, tn, col0):
    if col0 % tn == 0:
        joff = col0 // tn
        return pl.BlockSpec((k, tn), lambda i, j: (0, j + joff))
    assert col0 % LANES == 0 and tn % LANES == 0
    return pl.BlockSpec((pl.Element(k), pl.Element(tn)),
                        lambda i, j: (0, pl.multiple_of(col0 + j * tn, LANES)))


def _row_chunks(rows):
    step = min(rows, MM_ROW_CHUNK)
    assert rows % step == 0
    return [slice(r0, r0 + step) for r0 in range(0, rows, step)]


def _ffn_up_body(a_ref, wg_ref, wu_ref, o_ref):
    wg = wg_ref[...].astype(BF16)
    wu = wu_ref[...].astype(BF16)
    for rows in _row_chunks(a_ref.shape[0]):
        a = a_ref[rows, :]
        g = _dot(a, wg)
        u = _dot(a, wu)
        o_ref[rows, :] = (_silu(g) * u).astype(o_ref.dtype)


def _ffn_up(a, wg, wu):
    m, k = a.shape
    n = wg.shape[1]
    tm, tn = _tile(m, 2048), _tile(n, 256)
    return pl.pallas_call(
        _ffn_up_body,
        grid=(m // tm, n // tn),
        in_specs=[_a_spec(tm, k, True),
                  pl.BlockSpec((k, tn), lambda i, j: (0, j)),
                  pl.BlockSpec((k, tn), lambda i, j: (0, j))],
        out_specs=pl.BlockSpec((tm, tn), lambda i, j: (i, j)),
        out_shape=jax.ShapeDtypeStruct((m, n), BF16),
        compiler_params=_params("arbitrary", "arbitrary"),
        name="ffn_up",
    )(a, wg, wu)


def _mm_residual_body(a_ref, w_ref, r_ref, o_ref, *, scale):
    w = w_ref[...].astype(BF16)
    for rows in _row_chunks(a_ref.shape[0]):
        o_ref[rows, :] = r_ref[rows, :] + scale * _dot(a_ref[rows, :], w)


def _a_spec(tm, k, prefetch):
    if prefetch:
        return pl.BlockSpec((tm, k), lambda i, j: (i, 0))
    return _resident((tm, k), lambda i, j: (i, 0))


def _mm_residual(a, w, res, scale, tm_pref, tn_pref, prefetch_a=False):
    m, k = a.shape
    n = w.shape[1]
    tm, tn = _tile(m, tm_pref), _tile(n, tn_pref)
    return pl.pallas_call(
        functools.partial(_mm_residual_body, scale=scale),
        grid=(m // tm, n // tn),
        in_specs=[_a_spec(tm, k, prefetch_a),
                  pl.BlockSpec((k, tn), lambda i, j: (0, j)),
                  pl.BlockSpec((tm, tn), lambda i, j: (i, j))],
        out_specs=pl.BlockSpec((tm, tn), lambda i, j: (i, j)),
        out_shape=jax.ShapeDtypeStruct((m, n), F32),
        compiler_params=_params("arbitrary", "arbitrary"),
        name="mm_residual",
    )(a, w, res)


def _mm_plain_body(a_ref, w_ref, o_ref):
    w = w_ref[...].astype(BF16)
    for rows in _row_chunks(a_ref.shape[0]):
        o_ref[rows, :] = _dot(a_ref[rows, :], w).astype(o_ref.dtype)


def _mm_plain(a, w, col0, n, out_dtype):
    m, k = a.shape
    tm, tn = _tile(m, 2048), _tile(n, 512)
    return pl.pallas_call(
        _mm_plain_body,
        grid=(m // tm, n // tn),
        in_specs=[_resident((tm, k), lambda i, j: (i, 0)),
                  _w_spec(k, tn, col0)],
        out_specs=pl.BlockSpec((tm, tn), lambda i, j: (i, j)),
        out_shape=jax.ShapeDtypeStruct((m, n), out_dtype),
        compiler_params=_params("arbitrary", "arbitrary"),
        name="mm_plain",
    )(a, w)


def _mm_silu_side_body(a_ref, w_ref, ws_ref, o_ref, side_ref):
    w = w_ref[...].astype(BF16)
    for rows in _row_chunks(a_ref.shape[0]):
        o_ref[rows, :] = _silu(_dot(a_ref[rows, :], w)).astype(o_ref.dtype)

    @pl.when(pl.program_id(1) == 0)
    def _():
        ws = ws_ref[...].astype(BF16)
        for rows in _row_chunks(a_ref.shape[0]):
            side_ref[rows, :] = _dot(a_ref[rows, :], ws)


def _mm_silu_side(a, w, col0, n, side_col0, side_n):
    m, k = a.shape
    tm, tn = _tile(m, 2048), _tile(n, 512)
    assert side_col0 % side_n == 0 and side_n % LANES == 0
    side_blk = side_col0 // side_n
    return pl.pallas_call(
        _mm_silu_side_body,
        grid=(m // tm, n // tn),
        in_specs=[_resident((tm, k), lambda i, j: (i, 0)),
                  _w_spec(k, tn, col0),
                  _resident((k, side_n), lambda i, j: (0, side_blk))],
        out_specs=[pl.BlockSpec((tm, tn), lambda i, j: (i, j)),
                   pl.BlockSpec((tm, side_n), lambda i, j: (i, 0))],
        out_shape=[jax.ShapeDtypeStruct((m, n), BF16), jax.ShapeDtypeStruct((m, side_n), F32)],
        compiler_params=_params("arbitrary", "arbitrary"),
        name="mm_silu_side",
    )(a, w, w)


def _mm_conv_silu_body(a_ref, w_ref, cw_ref, cb_ref, o_ref, raw):
    w = w_ref[...].astype(BF16)
    taps = cw_ref[...]
    bias = cb_ref[...]
    kt = taps.shape[0]
    raw[0:SUBLANES, :] = jnp.zeros((SUBLANES, w.shape[1]), F32)
    for rows in _row_chunks(a_ref.shape[0]):
        raw[rows.start + SUBLANES:rows.stop + SUBLANES, :] = _dot(a_ref[rows, :], w)
        for r0 in range(rows.start, rows.stop, CONV_ROW_BLOCK):
            ext = raw[r0:r0 + CONV_ROW_BLOCK + SUBLANES, :]
            acc = ext[SUBLANES:] * taps[kt - 1:kt]
            for s in range(1, kt):
                acc = acc + pltpu.roll(ext, s, 0)[SUBLANES:] * taps[kt - 1 - s:kt - s]
            o_ref[r0:r0 + CONV_ROW_BLOCK, :] = _silu(acc + bias).astype(o_ref.dtype)


def _mm_conv_silu(a, w, col0, n, conv_w, conv_b, length):
    m, k = a.shape
    assert m % length == 0 and conv_w.shape[0] <= SUBLANES and min(length, MM_ROW_CHUNK) % CONV_ROW_BLOCK == 0
    tm, tn = length, _tile(n, 256)
    return pl.pallas_call(
        _mm_conv_silu_body,
        grid=(m // tm, n // tn),
        in_specs=[_a_spec(tm, k, True),
                  _w_spec(k, tn, col0),
                  pl.BlockSpec((conv_w.shape[0], tn), lambda i, j: (0, j)),
                  pl.BlockSpec((1, tn), lambda i, j: (0, j))],
        out_specs=pl.BlockSpec((tm, tn), lambda i, j: (i, j)),
        out_shape=jax.ShapeDtypeStruct((m, n), BF16),
        scratch_shapes=[pltpu.VMEM((tm + SUBLANES, tn), F32)],
        compiler_params=_params("arbitrary", "arbitrary"),
        name="mm_conv_silu",
    )(a, w, conv_w.astype(F32), conv_b.reshape(1, n).astype(F32))


def _mm_gate_body(a_ref, w_ref, g_ref, b_ref, o_ref):
    w = w_ref[...].astype(BF16)
    bias = b_ref[...]
    for rows in _row_chunks(a_ref.shape[0]):
        gate = jax.nn.sigmoid(g_ref[rows, :].astype(F32) + bias)
        o_ref[rows, :] = (gate * _dot(a_ref[rows, :], w)).astype(o_ref.dtype)


def _mm_gate_acc_body(a_ref, w_ref, g_ref, b_ref, p_ref, o_ref):
    w = w_ref[...].astype(BF16)
    bias = b_ref[...]
    for rows in _row_chunks(a_ref.shape[0]):
        gate = jax.nn.sigmoid(g_ref[rows, :].astype(F32) + bias)
        o_ref[rows, :] = (p_ref[rows, :] + gate * _dot(a_ref[rows, :], w)).astype(o_ref.dtype)


def _mm_gate(a, w, gpre, gcol0, bias, prev, out_dtype, tm_pref, tn_pref, prefetch_a=False):
    m, k = a.shape
    n = w.shape[1]
    tm, tn = _tile(m, tm_pref), _tile(n, tn_pref)
    joff = gcol0 // tn
    in_specs = [_a_spec(tm, k, prefetch_a),
                pl.BlockSpec((k, tn), lambda i, j: (0, j)),
                pl.BlockSpec((tm, tn), lambda i, j: (i, j + joff)),
                pl.BlockSpec((1, tn), lambda i, j: (0, j))]
    args = [a, w, gpre, bias.reshape(1, n).astype(F32)]
    body = _mm_gate_body
    if prev is not None:
        in_specs.append(pl.BlockSpec((tm, tn), lambda i, j: (i, j)))
        args.append(prev)
        body = _mm_gate_acc_body
    return pl.pallas_call(
        body,
        grid=(m // tm, n // tn),
        in_specs=in_specs,
        out_specs=pl.BlockSpec((tm, tn), lambda i, j: (i, j)),
        out_shape=jax.ShapeDtypeStruct((m, n), out_dtype),
        compiler_params=_params("arbitrary", "arbitrary"),
        name="mm_gate",
    )(*args)


def _mm_sconv_body(a_ref, wb_ref, wc_ref, wx_ref, cw_ref, o_ref, u_s, b_s):
    wb = wb_ref[...].astype(BF16)
    wc = wc_ref[...].astype(BF16)
    wx = wx_ref[...].astype(BF16)
    taps = cw_ref[...]
    kt = taps.shape[0]
    u_s[0:SUBLANES, :] = jnp.zeros((SUBLANES, wb.shape[1]), F32)
    for rows in _row_chunks(a_ref.shape[0]):
        a = a_ref[rows, :]
        u_s[rows.start + SUBLANES:rows.stop + SUBLANES, :] = _dot(a, wc) * _dot(a, wx)
        b_s[rows, :] = _dot(a, wb)
        for r0 in range(rows.start, rows.stop, CONV_ROW_BLOCK):
            ext = u_s[r0:r0 + CONV_ROW_BLOCK + SUBLANES, :]
            acc = ext[SUBLANES:] * taps[kt - 1:kt]
            for s in range(1, kt):
                acc = acc + pltpu.roll(ext, s, 0)[SUBLANES:] * taps[kt - 1 - s:kt - s]
            o_ref[r0:r0 + CONV_ROW_BLOCK, :] = (b_s[r0:r0 + CONV_ROW_BLOCK, :] * acc).astype(o_ref.dtype)


def _mm_sconv(a, w, width, conv_w, length):
    m, k = a.shape
    assert m % length == 0 and conv_w.shape[0] <= SUBLANES and min(length, MM_ROW_CHUNK) % CONV_ROW_BLOCK == 0
    tm, tn = length, _tile(width, 256)
    return pl.pallas_call(
        _mm_sconv_body,
        grid=(m // tm, width // tn),
        in_specs=[_a_spec(tm, k, False),
                  _w_spec(k, tn, 0),
                  _w_spec(k, tn, width),
                  _w_spec(k, tn, 2 * width),
                  pl.BlockSpec((conv_w.shape[0], tn), lambda i, j: (0, j))],
        out_specs=pl.BlockSpec((tm, tn), lambda i, j: (i, j)),
        out_shape=jax.ShapeDtypeStruct((m, width), BF16),
        scratch_shapes=[pltpu.VMEM((tm + SUBLANES, tn), F32), pltpu.VMEM((tm, tn), F32)],
        compiler_params=_params("arbitrary", "arbitrary"),
        name="mm_sconv",
    )(a, w, w, w, conv_w.astype(F32))


def _split_bf16(v, parts):
    out = []
    rem = v
    for _ in range(parts):
        p = rem.astype(BF16)
        out.append(p)
        rem = rem - p.astype(F32)
    return jnp.concatenate(out, axis=1)


def _ssd_body(xs_ref, bm_ref, cm_ref, zs_ref, dt_ref, dtb_ref, alog_ref, dx_ref, nw_ref, e1_ref, e2_ref,
              y_ref, state, wq_s, acs3_s, rowt_s, *, heads_per_group, groups_per_step):
    c = pl.program_id(1)
    gi = pl.program_id(2)
    t = SSM_CHUNK
    n = SSM_D_STATE
    r = heads_per_group
    rp = r * SSM_HEAD_DIM
    gps = groups_per_step
    heads_per_tile = LANES // SSM_HEAD_DIM

    @pl.when(c == 0)
    def _():
        for s in range(gps):
            state[gi * gps + s] = jnp.zeros(state.shape[1:], F32)

    @pl.when(gi == 0)
    def _():
        v = dt_ref[0] + dtb_ref[...]
        dt = jnp.maximum(v, 0.0) + jnp.log1p(jnp.exp(-jnp.abs(v)))
        a = dt * (-jnp.exp(alog_ref[...]))
        row = lax.broadcasted_iota(jnp.int32, a.shape, 0)
        acs = a
        s = 1
        while s < t:
            acs = acs + jnp.where(row >= s, pltpu.roll(acs, s, 0), 0.0)
            s *= 2
        eacs = jnp.exp(acs)
        wdec = dt * jnp.exp(acs[t - 1:t, :] - acs)
        wq_s[...] = _split_bf16(jnp.concatenate([wdec, eacs], axis=0), 2)
        acs2 = acs * LOG2_E
        acs3_s[...] = _split_bf16(acs2, 3)
        rowt = (acs2 - jnp.log2(dt)).T
        for gg in range(SSM_N_GROUPS):
            rowt_s[gg] = rowt[gg * r:(gg + 1) * r, :]

    tri = (lax.broadcasted_iota(jnp.int32, (t, t), 0) >= lax.broadcasted_iota(jnp.int32, (t, t), 1))
    lane = lax.broadcasted_iota(jnp.int32, (t, LANES), 1)
    wq = wq_s[...]
    acs3 = acs3_s[...]

    for s in range(gps):
        g = gi * gps + s
        ch = slice(s * rp, (s + 1) * rp)
        ex = _dot(wq, e1_ref[:, ch])
        wx, eacsx = ex[0:t], ex[t:2 * t]
        colb = _dot(acs3, e2_ref[:, s * r * t:(s + 1) * r * t])
        xs = xs_ref[0, :, ch].astype(F32)
        bmat = bm_ref[0, :, s * n:(s + 1) * n]
        cmat = cm_ref[0, :, s * n:(s + 1) * n]
        xdec_bf = (xs * wx).astype(BF16)
        cb = lax.dot_general(cmat, bmat, (((1,), (1,)), ((), ())), preferred_element_type=F32)
        sprev = state[g]
        yoff = _dot(cmat, sprev.astype(BF16)) * eacsx
        snew = lax.dot_general(bmat, xdec_bf, (((0,), (0,)), ((), ())), preferred_element_type=F32)
        state[g] = eacsx[t - 1:t, :] * sprev + snew

        tiles = []
        ss = jnp.zeros((t, 1), F32)
        for q in range(rp // LANES):
            sl = slice(q * LANES, (q + 1) * LANES)
            cs = slice(s * rp + q * LANES, s * rp + (q + 1) * LANES)
            xs_q = xs[:, sl]
            scores, xparts = [], []
            for e in range(heads_per_tile):
                k = q * heads_per_tile + e
                seg = colb[:, k * t:(k + 1) * t] - rowt_s[g, pl.ds(k, 1), :]
                scores.append((cb * jnp.exp2(jnp.where(tri, seg, -jnp.inf))).astype(BF16))
                in_head = (lane >= e * SSM_HEAD_DIM) & (lane < (e + 1) * SSM_HEAD_DIM)
                xparts.append(jnp.where(in_head, xs_q, 0.0).astype(BF16))
            ydiag = _dot(jnp.concatenate(scores, axis=1), jnp.concatenate(xparts, axis=0))
            acc = ydiag + yoff[:, sl] + dx_ref[:, cs] * xs_q
            gated = acc * zs_ref[0, :, cs].astype(F32)
            tiles.append(gated)
            ss = ss + jnp.sum(gated * gated, axis=-1, keepdims=True)
        inv = lax.rsqrt(ss / rp + RMS_EPS)
        for q, gated in enumerate(tiles):
            cs = slice(s * rp + q * LANES, s * rp + (q + 1) * LANES)
            y_ref[0, :, cs] = (gated * inv * nw_ref[:, cs]).astype(y_ref.dtype)


def _ssd(xbc, zs, dt_raw, dt_bias, a_log, d_skip, norm_w):
    bsz, length, d_inner = zs.shape
    n_heads = dt_raw.shape[-1]
    g_, n_, t = SSM_N_GROUPS, SSM_D_STATE, SSM_CHUNK
    r = n_heads // g_
    rp = r * SSM_HEAD_DIM
    gps = SSM_GROUPS_PER_STEP
    assert d_inner == n_heads * SSM_HEAD_DIM and rp % LANES == 0 and n_ == LANES and length % t == 0
    assert g_ % gps == 0
    wx, wn = gps * rp, gps * n_
    xb = d_inner // wn
    head_of_ch = jnp.arange(d_inner, dtype=jnp.int32) // SSM_HEAD_DIM
    e1 = (jnp.arange(n_heads, dtype=jnp.int32)[:, None] == head_of_ch[None, :]).astype(BF16)
    head_of_col = jnp.arange(n_heads * t, dtype=jnp.int32) // t
    e2 = (jnp.arange(n_heads, dtype=jnp.int32)[:, None] == head_of_col[None, :]).astype(BF16)
    e1 = jnp.tile(e1, (2, 1))
    e2 = jnp.tile(e2, (3, 1))
    d_x = jnp.repeat(d_skip.astype(F32), SSM_HEAD_DIM).reshape(1, d_inner)
    const_mode = pl.Buffered(1) if gps == g_ else None

    in_specs = [
        pl.BlockSpec((1, t, wx), lambda b, c, g: (b, c, g)),
        pl.BlockSpec((1, t, wn), lambda b, c, g: (b, c, xb + g)),
        pl.BlockSpec((1, t, wn), lambda b, c, g: (b, c, xb + g_ // gps + g)),
        pl.BlockSpec((1, t, wx), lambda b, c, g: (b, c, g)),
        pl.BlockSpec((1, t, n_heads), lambda b, c, g: (b, c, 0)),
        pl.BlockSpec((1, n_heads), lambda b, c, g: (0, 0)),
        pl.BlockSpec((1, n_heads), lambda b, c, g: (0, 0)),
        pl.BlockSpec((1, wx), lambda b, c, g: (0, g)),
        pl.BlockSpec((1, wx), lambda b, c, g: (0, g)),
        pl.BlockSpec((2 * n_heads, wx), lambda b, c, g: (0, g), pipeline_mode=const_mode),
        pl.BlockSpec((3 * n_heads, gps * r * t), lambda b, c, g: (0, g), pipeline_mode=const_mode),
    ]
    scratch = [
        pltpu.VMEM((g_, n_, rp), F32),
        pltpu.VMEM((2 * t, 2 * n_heads), BF16),
        pltpu.VMEM((t, 3 * n_heads), BF16),
        pltpu.VMEM((g_, r, t), F32),
    ]
    return pl.pallas_call(
        functools.partial(_ssd_body, heads_per_group=r, groups_per_step=gps),
        grid=(bsz, length // t, g_ // gps),
        in_specs=in_specs,
        out_specs=pl.BlockSpec((1, t, wx), lambda b, c, g: (b, c, g)),
        out_shape=jax.ShapeDtypeStruct((bsz, length, d_inner), BF16),
        scratch_shapes=scratch,
        compiler_params=_params("arbitrary", "arbitrary", "arbitrary"),
        name="ssd",
    )(xbc, xbc, xbc, zs, dt_raw,
      dt_bias.reshape(1, -1).astype(F32), a_log.reshape(1, -1).astype(F32), d_x,
      norm_w.reshape(1, -1).astype(F32), e1, e2)


def _swiglu_block(x2d, norm_w, w_gate, w_up, w_down):
    h = _rmsnorm(x2d, norm_w, BF16)
    act = _ffn_up(h, w_gate, w_up)
    return _mm_residual(act, w_down, x2d, FFN_RESIDUAL_SCALE, 1024, 256)


def _mixer_block(x2d, bsz, length, norm_w, w_in, gate_bias, sconv_w, sconv_w_out, ssm_conv_w, ssm_conv_b,
                 ssm_dt_bias, ssm_a_log, ssm_d, ssm_norm, ssm_w_out, w_o):
    d_model = x2d.shape[1]
    conv_width = sconv_w.shape[1]
    d_inner = ssm_norm.shape[0]
    conv_dim = ssm_conv_w.shape[1]
    n_heads = ssm_a_log.shape[0]
    o_z = 3 * conv_width
    o_xbc = o_z + d_inner
    o_dt = o_xbc + conv_dim
    o_g = o_dt + n_heads

    h = _rmsnorm(x2d, norm_w, BF16)
    v = _mm_sconv(h, w_in, conv_width, sconv_w, length)
    zs, p_dt = _mm_silu_side(h, w_in, o_z, d_inner, o_dt, n_heads)
    xbc = _mm_conv_silu(h, w_in, o_xbc, conv_dim, ssm_conv_w, ssm_conv_b, length)
    p_gate = _mm_plain(h, w_in, o_g, 2 * d_model, BF16)

    y = _ssd(xbc.reshape(bsz, length, conv_dim), zs.reshape(bsz, length, d_inner),
             p_dt.reshape(bsz, length, n_heads), ssm_dt_bias, ssm_a_log, ssm_d,
             ssm_norm).reshape(bsz * length, d_inner)

    merged = _mm_gate(v, sconv_w_out, p_gate, 0, gate_bias[:d_model], None, F32, 2048, 256, prefetch_a=True)
    merged = _mm_gate(y, ssm_w_out, p_gate, d_model, gate_bias[d_model:], merged, BF16, 1024, 256, prefetch_a=True)
    return _mm_residual(merged, w_o, x2d, 1.0, 2048, 256, prefetch_a=True)


def kernel(x, ffn1_norm, ffn1_w_gate, ffn1_w_up, ffn1_w_down, mix_norm, w_in, gate_bias, sconv_w, sconv_w_out, ssm_conv_w, ssm_conv_b, ssm_dt_bias, ssm_A_log, ssm_D, ssm_norm, ssm_w_out, w_o, ffn2_norm, ffn2_w_gate, ffn2_w_up, ffn2_w_down, final_norm):
    bsz, length, d_model = x.shape
    x2d = x.reshape(bsz * length, d_model)
    for i in range(ffn1_norm.shape[0]):
        x2d = _swiglu_block(x2d, ffn1_norm[i], ffn1_w_gate[i], ffn1_w_up[i], ffn1_w_down[i])
        x2d = _mixer_block(x2d, bsz, length, mix_norm[i], w_in[i], gate_bias[i], sconv_w[i], sconv_w_out[i],
                           ssm_conv_w[i], ssm_conv_b[i], ssm_dt_bias[i], ssm_A_log[i], ssm_D[i], ssm_norm[i],
                           ssm_w_out[i], w_o[i])
        x2d = _swiglu_block(x2d, ffn2_norm[i], ffn2_w_gate[i], ffn2_w_up[i], ffn2_w_down[i])
    return _rmsnorm(x2d, final_norm, x.dtype).reshape(bsz, length, d_model)
```

```python
import functools
import math

import jax
import jax.numpy as jnp
from jax import lax
from jax.experimental import pallas as pl
from jax.experimental.pallas import tpu as pltpu

F32 = jnp.float32
BF16 = jnp.bfloat16

RMS_EPS = 1e-5
FFN_RESIDUAL_SCALE = 0.5
SSM_N_GROUPS = 8
SSM_D_STATE = 128
SSM_HEAD_DIM = 64
SSM_CHUNK = 128
SSM_GROUPS_PER_STEP = 8
LOG2_E = math.log2(math.e)
LANES = 128
SUBLANES = 8
VMEM_LIMIT_BYTES = 56 * 1024 * 1024
MM_ROW_CHUNK = 512
CONV_ROW_BLOCK = 32


def _params(*sem):
    return pltpu.CompilerParams(dimension_semantics=sem, vmem_limit_bytes=VMEM_LIMIT_BYTES)


def _tile(n, pref):
    if n <= pref:
        return n
    t = (pref // LANES) * LANES
    while t >= LANES:
        if n % t == 0:
            return t
        t -= LANES
    return n


def _silu(v):
    return v * jax.nn.sigmoid(v)


def _dot(a, b):
    return jnp.dot(a, b, preferred_element_type=F32)


def _rmsnorm_body(x_ref, w_ref, o_ref):
    x = x_ref[...]
    ms = jnp.mean(x * x, axis=-1, keepdims=True)
    o_ref[...] = (x * lax.rsqrt(ms + RMS_EPS) * w_ref[...]).astype(o_ref.dtype)


def _rmsnorm(x2d, w, out_dtype):
    m, d = x2d.shape
    tm = _tile(m, 512)
    return pl.pallas_call(
        _rmsnorm_body,
        grid=(m // tm,),
        in_specs=[pl.BlockSpec((tm, d), lambda i: (i, 0)), pl.BlockSpec((1, d), lambda i: (0, 0))],
        out_specs=pl.BlockSpec((tm, d), lambda i: (i, 0)),
        out_shape=jax.ShapeDtypeStruct((m, d), out_dtype),
        compiler_params=_params("parallel"),
        name="rmsnorm",
    )(x2d, w.reshape(1, d).astype(F32))


def _resident(block, index_map):
    return pl.BlockSpec(block, index_map, pipeline_mode=pl.Buffered(1))


def _w_spec(k, tn, col0):
    if col0 % tn == 0:
        joff = col0 // tn
        return pl.BlockSpec((k, tn), lambda i, j: (0, j + joff))
    assert col0 % LANES == 0 and tn % LANES == 0
    return pl.BlockSpec((pl.Element(k), pl.Element(tn)),
                        lambda i, j: (0, pl.multiple_of(col0 + j * tn, LANES)))


def _row_chunks(rows):
    step = min(rows, MM_ROW_CHUNK)
    assert rows % step == 0
    return [slice(r0, r0 + step) for r0 in range(0, rows, step)]


def _ffn_up_body(a_ref, wg_ref, wu_ref, o_ref):
    wg = wg_ref[...].astype(BF16)
    wu = wu_ref[...].astype(BF16)
    for rows in _row_chunks(a_ref.shape[0]):
        a = a_ref[rows, :]
        g = _dot(a, wg)
        u = _dot(a, wu)
        o_ref[rows, :] = (_silu(g) * u).astype(o_ref.dtype)


def _ffn_up(a, wg, wu):
    m, k = a.shape
    n = wg.shape[1]
    tm, tn = _tile(m, 2048), _tile(n, 256)
    return pl.pallas_call(
        _ffn_up_body,
        grid=(m // tm, n // tn),
        in_specs=[_a_spec(tm, k, True),
                  pl.BlockSpec((k, tn), lambda i, j: (0, j)),
                  pl.BlockSpec((k, tn), lambda i, j: (0, j))],
        out_specs=pl.BlockSpec((tm, tn), lambda i, j: (i, j)),
        out_shape=jax.ShapeDtypeStruct((m, n), BF16),
        compiler_params=_params("arbitrary", "arbitrary"),
        name="ffn_up",
    )(a, wg, wu)


def _mm_residual_body(a_ref, w_ref, r_ref, o_ref, *, scale):
    w = w_ref[...].astype(BF16)
    for rows in _row_chunks(a_ref.shape[0]):
        o_ref[rows, :] = r_ref[rows, :] + scale * _dot(a_ref[rows, :], w)


def _a_spec(tm, k, prefetch):
    if prefetch:
        return pl.BlockSpec((tm, k), lambda i, j: (i, 0))
    return _resident((tm, k), lambda i, j: (i, 0))


def _mm_residual(a, w, res, scale, tm_pref, tn_pref, prefetch_a=False):
    m, k = a.shape
    n = w.shape[1]
    tm, tn = _tile(m, tm_pref), _tile(n, tn_pref)
    return pl.pallas_call(
        functools.partial(_mm_residual_body, scale=scale),
        grid=(m // tm, n // tn),
        in_specs=[_a_spec(tm, k, prefetch_a),
                  pl.BlockSpec((k, tn), lambda i, j: (0, j)),
                  pl.BlockSpec((tm, tn), lambda i, j: (i, j))],
        out_specs=pl.BlockSpec((tm, tn), lambda i, j: (i, j)),
        out_shape=jax.ShapeDtypeStruct((m, n), F32),
        compiler_params=_params("arbitrary", "arbitrary"),
        name="mm_residual",
    )(a, w, res)


def _mm_plain_body(a_ref, w_ref, o_ref):
    w = w_ref[...].astype(BF16)
    for rows in _row_chunks(a_ref.shape[0]):
        o_ref[rows, :] = _dot(a_ref[rows, :], w).astype(o_ref.dtype)


def _mm_plain(a, w, col0, n, out_dtype):
    m, k = a.shape
    tm, tn = _tile(m, 2048), _tile(n, 512)
    return pl.pallas_call(
        _mm_plain_body,
        grid=(m // tm, n // tn),
        in_specs=[_resident((tm, k), lambda i, j: (i, 0)),
                  _w_spec(k, tn, col0)],
        out_specs=pl.BlockSpec((tm, tn), lambda i, j: (i, j)),
        out_shape=jax.ShapeDtypeStruct((m, n), out_dtype),
        compiler_params=_params("arbitrary", "arbitrary"),
        name="mm_plain",
    )(a, w)


def _mm_silu_side_body(a_ref, w_ref, ws_ref, o_ref, side_ref):
    w = w_ref[...].astype(BF16)
    for rows in _row_chunks(a_ref.shape[0]):
        o_ref[rows, :] = _silu(_dot(a_ref[rows, :], w)).astype(o_ref.dtype)

    @pl.when(pl.program_id(1) == 0)
    def _():
        ws = ws_ref[...].astype(BF16)
        for rows in _row_chunks(a_ref.shape[0]):
            side_ref[rows, :] = _dot(a_ref[rows, :], ws)


def _mm_silu_side(a, w, col0, n, side_col0, side_n):
    m, k = a.shape
    tm, tn = _tile(m, 2048), _tile(n, 512)
    assert side_col0 % side_n == 0 and side_n % LANES == 0
    side_blk = side_col0 // side_n
    return pl.pallas_call(
        _mm_silu_side_body,
        grid=(m // tm, n // tn),
        in_specs=[_resident((tm, k), lambda i, j: (i, 0)),
                  _w_spec(k, tn, col0),
                  _resident((k, side_n), lambda i, j: (0, side_blk))],
        out_specs=[pl.BlockSpec((tm, tn), lambda i, j: (i, j)),
                   pl.BlockSpec((tm, side_n), lambda i, j: (i, 0))],
        out_shape=[jax.ShapeDtypeStruct((m, n), BF16), jax.ShapeDtypeStruct((m, side_n), F32)],
        compiler_params=_params("arbitrary", "arbitrary"),
        name="mm_silu_side",
    )(a, w, w)


def _mm_conv_silu_body(a_ref, w_ref, cw_ref, cb_ref, o_ref, raw):
    w = w_ref[...].astype(BF16)
    taps = cw_ref[...]
    bias = cb_ref[...]
    kt = taps.shape[0]
    raw[0:SUBLANES, :] = jnp.zeros((SUBLANES, w.shape[1]), F32)
    for rows in _row_chunks(a_ref.shape[0]):
        raw[rows.start + SUBLANES:rows.stop + SUBLANES, :] = _dot(a_ref[rows, :], w)
        for r0 in range(rows.start, rows.stop, CONV_ROW_BLOCK):
            ext = raw[r0:r0 + CONV_ROW_BLOCK + SUBLANES, :]
            acc = ext[SUBLANES:] * taps[kt - 1:kt]
            for s in range(1, kt):
                acc = acc + pltpu.roll(ext, s, 0)[SUBLANES:] * taps[kt - 1 - s:kt - s]
            o_ref[r0:r0 + CONV_ROW_BLOCK, :] = _silu(acc + bias).astype(o_ref.dtype)


def _mm_conv_silu(a, w, col0, n, conv_w, conv_b, length):
    m, k = a.shape
    assert m % length == 0 and conv_w.shape[0] <= SUBLANES and min(length, MM_ROW_CHUNK) % CONV_ROW_BLOCK == 0
    tm, tn = length, _tile(n, 512)
    return pl.pallas_call(
        _mm_conv_silu_body,
        grid=(m // tm, n // tn),
        in_specs=[_a_spec(tm, k, False),
                  _w_spec(k, tn, col0),
                  pl.BlockSpec((conv_w.shape[0], tn), lambda i, j: (0, j)),
                  pl.BlockSpec((1, tn), lambda i, j: (0, j))],
        out_specs=pl.BlockSpec((tm, tn), lambda i, j: (i, j)),
        out_shape=jax.ShapeDtypeStruct((m, n), BF16),
        scratch_shapes=[pltpu.VMEM((tm + SUBLANES, tn), F32)],
        compiler_params=_params("arbitrary", "arbitrary"),
        name="mm_conv_silu",
    )(a, w, conv_w.astype(F32), conv_b.reshape(1, n).astype(F32))


def _mm_gate_body(a_ref, w_ref, g_ref, b_ref, o_ref):
    w = w_ref[...].astype(BF16)
    bias = b_ref[...]
    for rows in _row_chunks(a_ref.shape[0]):
        gate = jax.nn.sigmoid(g_ref[rows, :].astype(F32) + bias)
        o_ref[rows, :] = (gate * _dot(a_ref[rows, :], w)).astype(o_ref.dtype)


def _mm_gate_acc_body(a_ref, w_ref, g_ref, b_ref, p_ref, o_ref):
    w = w_ref[...].astype(BF16)
    bias = b_ref[...]
    for rows in _row_chunks(a_ref.shape[0]):
        gate = jax.nn.sigmoid(g_ref[rows, :].astype(F32) + bias)
        o_ref[rows, :] = (p_ref[rows, :] + gate * _dot(a_ref[rows, :], w)).astype(o_ref.dtype)


def _mm_gate(a, w, gpre, gcol0, bias, prev, out_dtype, tm_pref, tn_pref, prefetch_a=False):
    m, k = a.shape
    n = w.shape[1]
    tm, tn = _tile(m, tm_pref), _tile(n, tn_pref)
    joff = gcol0 // tn
    in_specs = [_a_spec(tm, k, prefetch_a),
                pl.BlockSpec((k, tn), lambda i, j: (0, j)),
                pl.BlockSpec((tm, tn), lambda i, j: (i, j + joff)),
                pl.BlockSpec((1, tn), lambda i, j: (0, j))]
    args = [a, w, gpre, bias.reshape(1, n).astype(F32)]
    body = _mm_gate_body
    if prev is not None:
        in_specs.append(pl.BlockSpec((tm, tn), lambda i, j: (i, j)))
        args.append(prev)
        body = _mm_gate_acc_body
    return pl.pallas_call(
        body,
        grid=(m // tm, n // tn),
        in_specs=in_specs,
        out_specs=pl.BlockSpec((tm, tn), lambda i, j: (i, j)),
        out_shape=jax.ShapeDtypeStruct((m, n), out_dtype),
        compiler_params=_params("arbitrary", "arbitrary"),
        name="mm_gate",
    )(*args)


def _mm_sconv_body(a_ref, wb_ref, wc_ref, wx_ref, cw_ref, o_ref, u_s, b_s):
    wb = wb_ref[...].astype(BF16)
    wc = wc_ref[...].astype(BF16)
    wx = wx_ref[...].astype(BF16)
    taps = cw_ref[...]
    kt = taps.shape[0]
    u_s[0:SUBLANES, :] = jnp.zeros((SUBLANES, wb.shape[1]), F32)
    for rows in _row_chunks(a_ref.shape[0]):
        a = a_ref[rows, :]
        u_s[rows.start + SUBLANES:rows.stop + SUBLANES, :] = _dot(a, wc) * _dot(a, wx)
        b_s[rows, :] = _dot(a, wb)
        for r0 in range(rows.start, rows.stop, CONV_ROW_BLOCK):
            ext = u_s[r0:r0 + CONV_ROW_BLOCK + SUBLANES, :]
            acc = ext[SUBLANES:] * taps[kt - 1:kt]
            for s in range(1, kt):
                acc = acc + pltpu.roll(ext, s, 0)[SUBLANES:] * taps[kt - 1 - s:kt - s]
            o_ref[r0:r0 + CONV_ROW_BLOCK, :] = (b_s[r0:r0 + CONV_ROW_BLOCK, :] * acc).astype(o_ref.dtype)


def _mm_sconv(a, w, width, conv_w, length):
    m, k = a.shape
    assert m % length == 0 and conv_w.shape[0] <= SUBLANES and min(length, MM_ROW_CHUNK) % CONV_ROW_BLOCK == 0
    tm, tn = length, _tile(width, 256)
    return pl.pallas_call(
        _mm_sconv_body,
        grid=(m // tm, width // tn),
        in_specs=[_a_spec(tm, k, False),
                  _w_spec(k, tn, 0),
                  _w_spec(k, tn, width),
                  _w_spec(k, tn, 2 * width),
                  pl.BlockSpec((conv_w.shape[0], tn), lambda i, j: (0, j))],
        out_specs=pl.BlockSpec((tm, tn), lambda i, j: (i, j)),
        out_shape=jax.ShapeDtypeStruct((m, width), BF16),
        scratch_shapes=[pltpu.VMEM((tm + SUBLANES, tn), F32), pltpu.VMEM((tm, tn), F32)],
        compiler_params=_params("arbitrary", "arbitrary"),
        name="mm_sconv",
    )(a, w, w, w, conv_w.astype(F32))


def _split_bf16(v, parts):
    out = []
    rem = v
    for _ in range(parts):
        p = rem.astype(BF16)
        out.append(p)
        rem = rem - p.astype(F32)
    return jnp.concatenate(out, axis=1)


def _ssd_body(xs_ref, bm_ref, cm_ref, zs_ref, dt_ref, dtb_ref, alog_ref, dx_ref, nw_ref, e1_ref, e2_ref,
              y_ref, state, wq_s, acs3_s, rowt_s, *, heads_per_group, groups_per_step):
    c = pl.program_id(1)
    gi = pl.program_id(2)
    t = SSM_CHUNK
    n = SSM_D_STATE
    r = heads_per_group
    rp = r * SSM_HEAD_DIM
    gps = groups_per_step
    heads_per_tile = LANES // SSM_HEAD_DIM

    @pl.when(c == 0)
    def _():
        for s in range(gps):
            state[gi * gps + s] = jnp.zeros(state.shape[1:], F32)

    @pl.when(gi == 0)
    def _():
        v = dt_ref[0] + dtb_ref[...]
        dt = jnp.maximum(v, 0.0) + jnp.log1p(jnp.exp(-jnp.abs(v)))
        a = dt * (-jnp.exp(alog_ref[...]))
        row = lax.broadcasted_iota(jnp.int32, a.shape, 0)
        acs = a
        s = 1
        while s < t:
            acs = acs + jnp.where(row >= s, pltpu.roll(acs, s, 0), 0.0)
            s *= 2
        eacs = jnp.exp(acs)
        wdec = dt * jnp.exp(acs[t - 1:t, :] - acs)
        wq_s[...] = _split_bf16(jnp.concatenate([wdec, eacs], axis=0), 2)
        acs2 = acs * LOG2_E
        acs3_s[...] = _split_bf16(acs2, 3)
        rowt = (acs2 - jnp.log2(dt)).T
        for gg in range(SSM_N_GROUPS):
            rowt_s[gg] = rowt[gg * r:(gg + 1) * r, :]

    tri = (lax.broadcasted_iota(jnp.int32, (t, t), 0) >= lax.broadcasted_iota(jnp.int32, (t, t), 1))
    lane = lax.broadcasted_iota(jnp.int32, (t, LANES), 1)
    wq = wq_s[...]
    acs3 = acs3_s[...]

    for s in range(gps):
        g = gi * gps + s
        ch = slice(s * rp, (s + 1) * rp)
        ex = _dot(wq, e1_ref[:, ch])
        wx, eacsx = ex[0:t], ex[t:2 * t]
        colb = _dot(acs3, e2_ref[:, s * r * t:(s + 1) * r * t])
        xs = xs_ref[0, :, ch].astype(F32)
        bmat = bm_ref[0, :, s * n:(s + 1) * n]
        cmat = cm_ref[0, :, s * n:(s + 1) * n]
        xdec_bf = (xs * wx).astype(BF16)
        cb = lax.dot_general(cmat, bmat, (((1,), (1,)), ((), ())), preferred_element_type=F32)
        sprev = state[g]
        yoff = _dot(cmat, sprev.astype(BF16)) * eacsx
        snew = lax.dot_general(bmat, xdec_bf, (((0,), (0,)), ((), ())), preferred_element_type=F32)
        state[g] = eacsx[t - 1:t, :] * sprev + snew

        tiles = []
        ss = jnp.zeros((t, 1), F32)
        for q in range(rp // LANES):
            sl = slice(q * LANES, (q + 1) * LANES)
            cs = slice(s * rp + q * LANES, s * rp + (q + 1) * LANES)
            xs_q = xs[:, sl]
            scores, xparts = [], []
            for e in range(heads_per_tile):
                k = q * heads_per_tile + e
                seg = colb[:, k * t:(k + 1) * t] - rowt_s[g, pl.ds(k, 1), :]
                scores.append((cb * jnp.exp2(jnp.where(tri, seg, -jnp.inf))).astype(BF16))
                in_head = (lane >= e * SSM_HEAD_DIM) & (lane < (e + 1) * SSM_HEAD_DIM)
                xparts.append(jnp.where(in_head, xs_q, 0.0).astype(BF16))
            ydiag = _dot(jnp.concatenate(scores, axis=1), jnp.concatenate(xparts, axis=0))
            acc = ydiag + yoff[:, sl] + dx_ref[:, cs] * xs_q
            gated = acc * zs_ref[0, :, cs].astype(F32)
            tiles.append(gated)
            ss = ss + jnp.sum(gated * gated, axis=-1, keepdims=True)
        inv = lax.rsqrt(ss / rp + RMS_EPS)
        for q, gated in enumerate(tiles):
            cs = slice(s * rp + q * LANES, s * rp + (q + 1) * LANES)
            y_ref[0, :, cs] = (gated * inv * nw_ref[:, cs]).astype(y_ref.dtype)


def _ssd(xbc, zs, dt_raw, dt_bias, a_log, d_skip, norm_w):
    bsz, length, d_inner = zs.shape
    n_heads = dt_raw.shape[-1]
    g_, n_, t = SSM_N_GROUPS, SSM_D_STATE, SSM_CHUNK
    r = n_heads // g_
    rp = r * SSM_HEAD_DIM
    gps = SSM_GROUPS_PER_STEP
    assert d_inner == n_heads * SSM_HEAD_DIM and rp % LANES == 0 and n_ == LANES and length % t == 0
    assert g_ % gps == 0
    wx, wn = gps * rp, gps * n_
    xb = d_inner // wn
    head_of_ch = jnp.arange(d_inner, dtype=jnp.int32) // SSM_HEAD_DIM
    e1 = (jnp.arange(n_heads, dtype=jnp.int32)[:, None] == head_of_ch[None, :]).astype(BF16)
    head_of_col = jnp.arange(n_heads * t, dtype=jnp.int32) // t
    e2 = (jnp.arange(n_heads, dtype=jnp.int32)[:, None] == head_of_col[None, :]).astype(BF16)
    e1 = jnp.tile(e1, (2, 1))
    e2 = jnp.tile(e2, (3, 1))
    d_x = jnp.repeat(d_skip.astype(F32), SSM_HEAD_DIM).reshape(1, d_inner)
    const_mode = pl.Buffered(1) if gps == g_ else None

    in_specs = [
        pl.BlockSpec((1, t, wx), lambda b, c, g: (b, c, g)),
        pl.BlockSpec((1, t, wn), lambda b, c, g: (b, c, xb + g)),
        pl.BlockSpec((1, t, wn), lambda b, c, g: (b, c, xb + g_ // gps + g)),
        pl.BlockSpec((1, t, wx), lambda b, c, g: (b, c, g)),
        pl.BlockSpec((1, t, n_heads), lambda b, c, g: (b, c, 0)),
        pl.BlockSpec((1, n_heads), lambda b, c, g: (0, 0)),
        pl.BlockSpec((1, n_heads), lambda b, c, g: (0, 0)),
        pl.BlockSpec((1, wx), lambda b, c, g: (0, g)),
        pl.BlockSpec((1, wx), lambda b, c, g: (0, g)),
        pl.BlockSpec((2 * n_heads, wx), lambda b, c, g: (0, g), pipeline_mode=const_mode),
        pl.BlockSpec((3 * n_heads, gps * r * t), lambda b, c, g: (0, g), pipeline_mode=const_mode),
    ]
    scratch = [
        pltpu.VMEM((g_, n_, rp), F32),
        pltpu.VMEM((2 * t, 2 * n_heads), BF16),
        pltpu.VMEM((t, 3 * n_heads), BF16),
        pltpu.VMEM((g_, r, t), F32),
    ]
    return pl.pallas_call(
        functools.partial(_ssd_body, heads_per_group=r, groups_per_step=gps),
        grid=(bsz, length // t, g_ // gps),
        in_specs=in_specs,
        out_specs=pl.BlockSpec((1, t, wx), lambda b, c, g: (b, c, g)),
        out_shape=jax.ShapeDtypeStruct((bsz, length, d_inner), BF16),
        scratch_shapes=scratch,
        compiler_params=_params("arbitrary", "arbitrary", "arbitrary"),
        name="ssd",
    )(xbc, xbc, xbc, zs, dt_raw,
      dt_bias.reshape(1, -1).astype(F32), a_log.reshape(1, -1).astype(F32), d_x,
      norm_w.reshape(1, -1).astype(F32), e1, e2)


def _swiglu_block(x2d, norm_w, w_gate, w_up, w_down):
    h = _rmsnorm(x2d, norm_w, BF16)
    act = _ffn_up(h, w_gate, w_up)
    return _mm_residual(act, w_down, x2d, FFN_RESIDUAL_SCALE, 1024, 256)


def _mixer_block(x2d, bsz, length, norm_w, w_in, gate_bias, sconv_w, sconv_w_out, ssm_conv_w, ssm_conv_b,
                 ssm_dt_bias, ssm_a_log, ssm_d, ssm_norm, ssm_w_out, w_o):
    d_model = x2d.shape[1]
    conv_width = sconv_w.shape[1]
    d_inner = ssm_norm.shape[0]
    conv_dim = ssm_conv_w.shape[1]
    n_heads = ssm_a_log.shape[0]
    o_z = 3 * conv_width
    o_xbc = o_z + d_inner
    o_dt = o_xbc + conv_dim
    o_g = o_dt + n_heads

    h = _rmsnorm(x2d, norm_w, BF16)
    v = _mm_sconv(h, w_in, conv_width, sconv_w, length)
    zs, p_dt = _mm_silu_side(h, w_in, o_z, d_inner, o_dt, n_heads)
    xbc = _mm_conv_silu(h, w_in, o_xbc, conv_dim, ssm_conv_w, ssm_conv_b, length)
    p_gate = _mm_plain(h, w_in, o_g, 2 * d_model, BF16)

    y = _ssd(xbc.reshape(bsz, length, conv_dim), zs.reshape(bsz, length, d_inner),
             p_dt.reshape(bsz, length, n_heads), ssm_dt_bias, ssm_a_log, ssm_d,
             ssm_norm).reshape(bsz * length, d_inner)

    merged = _mm_gate(v, sconv_w_out, p_gate, 0, gate_bias[:d_model], None, F32, 2048, 256, prefetch_a=True)
    merged = _mm_gate(y, ssm_w_out, p_gate, d_model, gate_bias[d_model:], merged, BF16, 1024, 256, prefetch_a=True)
    return _mm_residual(merged, w_o, x2d, 1.0, 2048, 256, prefetch_a=True)


def kernel(x, ffn1_norm, ffn1_w_gate, ffn1_w_up, ffn1_w_down, mix_norm, w_in, gate_bias, sconv_w, sconv_w_out, ssm_conv_w, ssm_conv_b, ssm_dt_bias, ssm_A_log, ssm_D, ssm_norm, ssm_w_out, w_o, ffn2_norm, ffn2_w_gate, ffn2_w_up, ffn2_w_down, final_norm):
    bsz, length, d_model = x.shape
    x2d = x.reshape(bsz * length, d_model)
    for i in range(ffn1_norm.shape[0]):
        x2d = _swiglu_block(x2d, ffn1_norm[i], ffn1_w_gate[i], ffn1_w_up[i], ffn1_w_down[i])
        x2d = _mixer_block(x2d, bsz, length, mix_norm[i], w_in[i], gate_bias[i], sconv_w[i], sconv_w_out[i],
                           ssm_conv_w[i], ssm_conv_b[i], ssm_dt_bias[i], ssm_A_log[i], ssm_D[i], ssm_norm[i],
                           ssm_w_out[i], w_o[i])
        x2d = _swiglu_block(x2d, ffn2_norm[i], ffn2_w_gate[i], ffn2_w_up[i], ffn2_w_down[i])
    return _rmsnorm(x2d, final_norm, x.dtype).reshape(bsz, length, d_model)
```
